```python
import math
import jax
import jax.numpy as jnp
from jax import lax
import numpy as np

D_MODEL = 1024
BATCH = 8
SEQ = 2048
DEPTH = 4
DEC_BATCH = 16
DEC_SEQ = 64
PAST_LEN = 1024

CHUNK = 64
N_MIXERS = 3
EPS = 1e-6

SSD_D_INNER = 2 * D_MODEL
SSD_HEAD_DIM = 64
SSD_N_HEADS = SSD_D_INNER // SSD_HEAD_DIM
SSD_N_GROUPS = 8
SSD_HEADS_PER_GROUP = SSD_N_HEADS // SSD_N_GROUPS
SSD_D_STATE = 128
SSD_CONV_W = 4
SSD_CONV_DIM = SSD_D_INNER + 2 * SSD_N_GROUPS * SSD_D_STATE
SSD_IN_DIM = SSD_D_INNER + SSD_CONV_DIM + SSD_N_HEADS

ATT_N_HEADS = 16
ATT_N_KV_HEADS = 4
ATT_HEAD_DIM = 64
ATT_Q_DIM = ATT_N_HEADS * ATT_HEAD_DIM
ATT_KV_DIM = ATT_N_KV_HEADS * ATT_HEAD_DIM
IDX_N_HEADS = 8
IDX_HEAD_DIM = 64
IDX_Q_DIM = IDX_N_HEADS * IDX_HEAD_DIM
DSA_SPLITS = (ATT_Q_DIM,
              ATT_Q_DIM + ATT_KV_DIM,
              ATT_Q_DIM + 2 * ATT_KV_DIM,
              ATT_Q_DIM + 2 * ATT_KV_DIM + IDX_Q_DIM,
              ATT_Q_DIM + 2 * ATT_KV_DIM + IDX_Q_DIM + IDX_HEAD_DIM)
DSA_IN_DIM = DSA_SPLITS[-1] + IDX_N_HEADS
TOPK_MAX = 256
Q_BLOCK = 128

SC_WIDTH = 3

FFN_HIDDEN = -(-(8 * D_MODEL) // (3 * 256)) * 256

kernel_name = 'hybrid_ssd_dsa_shortconv_stream_step'


def rmsnorm(x, g):
    xf = x.astype(jnp.float32)
    y = xf * lax.rsqrt(jnp.mean(xf * xf, axis=-1, keepdims=True) + EPS)
    return (y * g.astype(jnp.float32)).astype(x.dtype)


def causal_depthwise_conv(u, state, w):
    width, length = w.shape[0], u.shape[1]
    up = jnp.concatenate([state.astype(u.dtype), u], axis=1)
    y = up[:, 0:length] * w[0]
    for k in range(1, width):
        y = y + up[:, k:k + length] * w[k]
    return y, up[:, up.shape[1] - (width - 1):]


def ssd_scan(x, dt, a_log, b_mat, c_mat, h0):
    bsz, length = x.shape[0], x.shape[1]
    q = min(CHUNK, length)
    nc = length // q
    g, r, p, n = SSD_N_GROUPS, SSD_HEADS_PER_GROUP, SSD_HEAD_DIM, SSD_D_STATE
    a_neg = -jnp.exp(a_log.astype(jnp.float32)).reshape(g, r)
    xc = x.astype(jnp.float32).reshape(bsz, nc, q, g, r, p)
    dtc = dt.reshape(bsz, nc, q, g, r)
    bc = b_mat.astype(jnp.float32).reshape(bsz, nc, q, g, n)
    cc = c_mat.astype(jnp.float32).reshape(bsz, nc, q, g, n)
    cum = jnp.cumsum(dtc * a_neg, axis=2)
    causal = jnp.tril(jnp.ones((q, q), dtype=bool))[:, :, None, None]
    seg = cum[:, :, :, None] - cum[:, :, None, :]
    decay = jnp.exp(jnp.where(causal, seg, -jnp.inf))
    cb = jnp.einsum('bcign,bcjgn->bcijg', cc, bc)
    mix = cb[..., None] * decay * dtc[:, :, None]
    y_intra = jnp.einsum('bcijgr,bcjgrp->bcigrp', mix, xc)
    w_end = jnp.exp(cum[:, :, -1:] - cum) * dtc
    chunk_states = jnp.einsum('bcjgn,bcjgr,bcjgrp->bcgrpn', bc, w_end, xc)
    chunk_decay = jnp.exp(cum[:, :, -1])

    def step(h, inp):
        st, dec = inp
        return dec[..., None, None] * h + st, h

    h_last, h_prev = lax.scan(step, h0.astype(jnp.float32).reshape(bsz, g, r, p, n),
                              (jnp.moveaxis(chunk_states, 1, 0), jnp.moveaxis(chunk_decay, 1, 0)))
    h_prev = jnp.moveaxis(h_prev, 0, 1)
    y_inter = jnp.einsum('bcign,bcgrpn->bcigrp', cc, h_prev) * jnp.exp(cum)[..., None]
    y = (y_intra + y_inter).reshape(bsz, length, SSD_N_HEADS, p)
    return y, h_last.reshape(bsz, SSD_N_HEADS, p, n)


def ssd_mixer(x, conv_state, ssm_state, w_in, conv_w, conv_b, dt_bias, a_log, d_skip, norm_g, w_out):
    bsz, length, _ = x.shape
    z, xbc, dt_raw = jnp.split(x @ w_in, [SSD_D_INNER, SSD_D_INNER + SSD_CONV_DIM], axis=-1)
    xbc, new_conv = causal_depthwise_conv(xbc, conv_state, conv_w)
    xbc = jax.nn.silu(xbc + conv_b)
    xs, b_mat, c_mat = jnp.split(xbc, [SSD_D_INNER, SSD_D_INNER + SSD_N_GROUPS * SSD_D_STATE], axis=-1)
    xs = xs.reshape(bsz, length, SSD_N_HEADS, SSD_HEAD_DIM)
    b_mat = b_mat.reshape(bsz, length, SSD_N_GROUPS, SSD_D_STATE)
    c_mat = c_mat.reshape(bsz, length, SSD_N_GROUPS, SSD_D_STATE)
    dt = jax.nn.softplus(dt_raw.astype(jnp.float32) + dt_bias.astype(jnp.float32))
    y, new_ssm = ssd_scan(xs, dt, a_log, b_mat, c_mat, ssm_state)
    y = y + d_skip.astype(jnp.float32)[:, None] * xs.astype(jnp.float32)
    y = y.reshape(bsz, length, SSD_D_INNER) * jax.nn.silu(z.astype(jnp.float32))
    yg = y.reshape(bsz, length, SSD_N_GROUPS, SSD_D_INNER // SSD_N_GROUPS)
    yg = yg * lax.rsqrt(jnp.mean(yg * yg, axis=-1, keepdims=True) + EPS)
    y = (yg.reshape(bsz, length, SSD_D_INNER) * norm_g.astype(jnp.float32)).astype(x.dtype)
    return y @ w_out, new_conv, new_ssm.astype(x.dtype)


def dsa_select_attend(q, qi, wi, q_pos, k_all, v_all, ki_all, k_pos, topk):
    bsz, t = q.shape[0], q.shape[1]
    dots = jnp.einsum('bthd,bsd->bths', qi.astype(jnp.float32), ki_all.astype(jnp.float32))
    score = jnp.einsum('bth,bths->bts', wi.astype(jnp.float32), jax.nn.relu(dots))
    admissible = (k_pos[None, :] // CHUNK) <= (q_pos[:, None] // CHUNK)
    score = jnp.where(admissible[None], score, -jnp.inf)
    top_val, top_idx = lax.top_k(score, topk)
    valid = top_val > -jnp.inf
    k_sel = jax.vmap(lambda kv, idx: kv[idx])(k_all, top_idx)
    v_sel = jax.vmap(lambda kv, idx: kv[idx])(v_all, top_idx)
    qg = q.astype(jnp.float32).reshape(bsz, t, ATT_N_KV_HEADS, ATT_N_HEADS // ATT_N_KV_HEADS, ATT_HEAD_DIM)
    logits = jnp.einsum('btgrd,btkgd->btgrk', qg, k_sel.astype(jnp.float32)) * (ATT_HEAD_DIM ** -0.5)
    logits = jnp.where(valid[:, :, None, None, :], logits, -jnp.inf)
    prob = jax.nn.softmax(logits, axis=-1)
    o = jnp.einsum('btgrk,btkgd->btgrd', prob, v_sel.astype(jnp.float32))
    return o.reshape(bsz, t, ATT_Q_DIM).astype(q.dtype)


def dsa_mixer(x, cache_k, cache_v, cache_kidx, w_in, w_out):
    bsz, length, _ = x.shape
    q, k, v, qi, ki, wi = jnp.split(x @ w_in, list(DSA_SPLITS), axis=-1)
    q = q.reshape(bsz, length, ATT_N_HEADS, ATT_HEAD_DIM)
    k = k.reshape(bsz, length, ATT_N_KV_HEADS, ATT_HEAD_DIM)
    v = v.reshape(bsz, length, ATT_N_KV_HEADS, ATT_HEAD_DIM)
    qi = qi.reshape(bsz, length, IDX_N_HEADS, IDX_HEAD_DIM) * (IDX_HEAD_DIM ** -0.5)
    wi = wi * (IDX_N_HEADS ** -0.5)
    past = cache_k.shape[1]
    k_all = jnp.concatenate([cache_k.astype(k.dtype), k], axis=1)
    v_all = jnp.concatenate([cache_v.astype(v.dtype), v], axis=1)
    ki_all = jnp.concatenate([cache_kidx.astype(ki.dtype), ki], axis=1)
    n_keys = past + length
    topk = min(TOPK_MAX, n_keys // 4)
    k_pos = jnp.arange(n_keys)
    q_pos = past + jnp.arange(length)
    qb = min(Q_BLOCK, length)
    nb = length // qb

    def blocks(a):
        return jnp.moveaxis(a.reshape((bsz, nb, qb) + a.shape[2:]), 1, 0)

    def attend_block(args):
        q_b, qi_b, wi_b, pos_b = args
        return dsa_select_attend(q_b, qi_b, wi_b, pos_b, k_all, v_all, ki_all, k_pos, topk)

    o = lax.map(attend_block, (blocks(q), blocks(qi), blocks(wi), q_pos.reshape(nb, qb)))
    o = jnp.moveaxis(o, 0, 1).reshape(bsz, length, ATT_Q_DIM)
    return o @ w_out, k, v, ki


def shortconv_mixer(x, conv_state, w_in, conv_w, w_out):
    gate_b, gate_c, u = jnp.split(x @ w_in, 3, axis=-1)
    y, new_conv = causal_depthwise_conv(gate_c * u, conv_state, conv_w)
    return (gate_b * y) @ w_out, new_conv


def swiglu(x, w_gate, w_up, w_down):
    return (jax.nn.silu(x @ w_gate) * (x @ w_up)) @ w_down


def fresh_state(kind, bsz, dtype):
    if kind == 0:
        return (jnp.zeros((bsz, SSD_CONV_W - 1, SSD_CONV_DIM), dtype),
                jnp.zeros((bsz, SSD_N_HEADS, SSD_HEAD_DIM, SSD_D_STATE), jnp.float32))
    if kind == 1:
        return (jnp.zeros((bsz, 0, ATT_N_KV_HEADS, ATT_HEAD_DIM), dtype),
                jnp.zeros((bsz, 0, ATT_N_KV_HEADS, ATT_HEAD_DIM), dtype),
                jnp.zeros((bsz, 0, IDX_HEAD_DIM), dtype))
    return (jnp.zeros((bsz, SC_WIDTH - 1, D_MODEL), dtype),)


def trunk_layer(kind, h, st, mix_params, g_mix, g_ffn, w_gate, w_up, w_down):
    u = rmsnorm(h, g_mix)
    if kind == 0:
        out = ssd_mixer(u, *st, *mix_params)
    elif kind == 1:
        out = dsa_mixer(u, *st, *mix_params)
    else:
        out = shortconv_mixer(u, *st, *mix_params)
    h = h + out[0]
    h = h + swiglu(rmsnorm(h, g_ffn), w_gate, w_up, w_down)
    return h, out[1:]


def _ssd_params(key, prefix):
    ks = jax.random.split(key, 8)
    dt = jnp.exp(jax.random.uniform(ks[2], (SSD_N_HEADS,), jnp.float32,
                                    minval=math.log(1e-3), maxval=math.log(1e-1)))
    return {
        prefix + 'w_in': jax.random.normal(ks[0], (D_MODEL, SSD_IN_DIM), jnp.float32) * D_MODEL ** -0.5,
        prefix + 'conv_w': jax.random.normal(ks[1], (SSD_CONV_W, SSD_CONV_DIM), jnp.float32) * SSD_CONV_W ** -0.5,
        prefix + 'conv_b': 0.01 * jax.random.normal(ks[3], (SSD_CONV_DIM,), jnp.float32),
        prefix + 'dt_bias': dt + jnp.log(-jnp.expm1(-dt)),
        prefix + 'a_log': jnp.log(jax.random.uniform(ks[4], (SSD_N_HEADS,), jnp.float32, minval=1.0, maxval=16.0)),
        prefix + 'd_skip': 1.0 + 0.1 * jax.random.normal(ks[5], (SSD_N_HEADS,), jnp.float32),
        prefix + 'norm_g': 1.0 + 0.01 * jax.random.normal(ks[6], (SSD_D_INNER,), jnp.float32),
        prefix + 'w_out': jax.random.normal(ks[7], (SSD_D_INNER, D_MODEL), jnp.float32) * SSD_D_INNER ** -0.5,
    }


def setup_inputs(seed: int = 0) -> dict:
    key = jax.random.key(seed)
    ks = jax.random.split(key, 24)

    def nrm(k, shape, scale=1.0):
        return scale * jax.random.normal(k, shape, jnp.float32)

    inputs = {
        'x_prompt': nrm(ks[0], (BATCH, SEQ, D_MODEL)),
        'x_sample': nrm(ks[1], (DEC_BATCH, DEC_SEQ, D_MODEL)),
        'state_l0_conv': nrm(ks[2], (DEC_BATCH, SSD_CONV_W - 1, SSD_CONV_DIM)),
        'state_l0_ssm': nrm(ks[3], (DEC_BATCH, SSD_N_HEADS, SSD_HEAD_DIM, SSD_D_STATE), 0.3),
        'cache_l1_k': nrm(ks[4], (DEC_BATCH, PAST_LEN, ATT_N_KV_HEADS, ATT_HEAD_DIM)),
        'cache_l1_v': nrm(ks[5], (DEC_BATCH, PAST_LEN, ATT_N_KV_HEADS, ATT_HEAD_DIM)),
        'cache_l1_kidx': nrm(ks[6], (DEC_BATCH, PAST_LEN, IDX_HEAD_DIM)),
        'state_l2_conv': nrm(ks[7], (DEC_BATCH, SC_WIDTH - 1, D_MODEL)),
        'state_l3_conv': nrm(ks[8], (DEC_BATCH, SSD_CONV_W - 1, SSD_CONV_DIM)),
        'state_l3_ssm': nrm(ks[9], (DEC_BATCH, SSD_N_HEADS, SSD_HEAD_DIM, SSD_D_STATE), 0.3),
        'norm_mix_g': 1.0 + nrm(ks[10], (DEPTH, D_MODEL), 0.01),
        'norm_ffn_g': 1.0 + nrm(ks[11], (DEPTH, D_MODEL), 0.01),
        'norm_final_g': 1.0 + nrm(ks[12], (D_MODEL,), 0.01),
        'ffn_w_gate': nrm(ks[13], (DEPTH, D_MODEL, FFN_HIDDEN), D_MODEL ** -0.5),
        'ffn_w_up': nrm(ks[14], (DEPTH, D_MODEL, FFN_HIDDEN), D_MODEL ** -0.5),
        'ffn_w_down': nrm(ks[15], (DEPTH, FFN_HIDDEN, D_MODEL), FFN_HIDDEN ** -0.5),
    }
    inputs.update(_ssd_params(ks[16], 'l0_'))
    inputs['l1_w_in'] = nrm(ks[17], (D_MODEL, DSA_IN_DIM), D_MODEL ** -0.5)
    inputs['l1_w_out'] = nrm(ks[18], (ATT_Q_DIM, D_MODEL), ATT_Q_DIM ** -0.5)
    inputs['l2_w_in'] = nrm(ks[19], (D_MODEL, 3 * D_MODEL), D_MODEL ** -0.5)
    inputs['l2_conv_w'] = nrm(ks[20], (SC_WIDTH, D_MODEL), SC_WIDTH ** -0.5)
    inputs['l2_w_out'] = nrm(ks[21], (D_MODEL, D_MODEL), D_MODEL ** -0.5)
    inputs.update(_ssd_params(ks[22], 'l3_'))
    return inputs


def reference(x_prompt, x_sample,
              state_l0_conv, state_l0_ssm, cache_l1_k, cache_l1_v, cache_l1_kidx,
              state_l2_conv, state_l3_conv, state_l3_ssm,
              norm_mix_g, norm_ffn_g, norm_final_g, ffn_w_gate, ffn_w_up, ffn_w_down,
              l0_w_in, l0_conv_w, l0_conv_b, l0_dt_bias, l0_a_log, l0_d_skip, l0_norm_g, l0_w_out,
              l1_w_in, l1_w_out,
              l2_w_in, l2_conv_w, l2_w_out,
              l3_w_in, l3_conv_w, l3_conv_b, l3_dt_bias, l3_a_log, l3_d_skip, l3_norm_g, l3_w_out):
    mix_params = (
        (l0_w_in, l0_conv_w, l0_conv_b, l0_dt_bias, l0_a_log, l0_d_skip, l0_norm_g, l0_w_out),
        (l1_w_in, l1_w_out),
        (l2_w_in, l2_conv_w, l2_w_out),
        (l3_w_in, l3_conv_w, l3_conv_b, l3_dt_bias, l3_a_log, l3_d_skip, l3_norm_g, l3_w_out),
    )
    sample_states = (
        (state_l0_conv, state_l0_ssm),
        (cache_l1_k, cache_l1_v, cache_l1_kidx),
        (state_l2_conv,),
        (state_l3_conv, state_l3_ssm),
    )
    hp, hs = x_prompt, x_sample
    new_p, new_s = [], []
    for i in range(DEPTH):
        kind = i % N_MIXERS
        ffn = (norm_mix_g[i], norm_ffn_g[i], ffn_w_gate[i], ffn_w_up[i], ffn_w_down[i])
        hp, st_p = trunk_layer(kind, hp, fresh_state(kind, hp.shape[0], hp.dtype), mix_params[i], *ffn)
        hs, st_s = trunk_layer(kind, hs, sample_states[i], mix_params[i], *ffn)
        new_p.append(st_p)
        new_s.append(st_s)
    y_prompt = rmsnorm(hp, norm_final_g)
    y_sample = rmsnorm(hs, norm_final_g)
    (p_l0_conv, p_l0_ssm), (p_l1_k, p_l1_v, p_l1_kidx), (p_l2_conv,), (p_l3_conv, p_l3_ssm) = new_p
    (s_l0_conv, s_l0_ssm), (s_l1_k, s_l1_v, s_l1_kidx), (s_l2_conv,), (s_l3_conv, s_l3_ssm) = new_s
    return (y_prompt, y_sample,
            p_l0_conv, p_l0_ssm, s_l0_conv, s_l0_ssm,
            p_l1_k, p_l1_v, p_l1_kidx, s_l1_k, s_l1_v, s_l1_kidx,
            p_l2_conv, s_l2_conv,
            p_l3_conv, p_l3_ssm, s_l3_conv, s_l3_ssm)
```

```python
import functools

import jax
import jax.numpy as jnp
from jax import lax
from jax.experimental import pallas as pl
from jax.experimental.pallas import tpu as pltpu

F32 = jnp.float32
BF16 = jnp.bfloat16

EPS = 1e-6
CHUNK = 64

D_MODEL = 1024
FFN_HIDDEN = 2816

SSD_D_INNER = 2048
SSD_HEAD_DIM = 64
SSD_N_HEADS = 32
SSD_N_GROUPS = 8
SSD_HEADS_PER_GROUP = 4
SSD_D_STATE = 128
SSD_CONV_W = 4
SSD_BC_DIM = SSD_N_GROUPS * SSD_D_STATE
SSD_CONV_DIM = SSD_D_INNER + 2 * SSD_BC_DIM
SSD_IN_DIM = SSD_D_INNER + SSD_CONV_DIM + SSD_N_HEADS
SSD_IN_PAD = 6400
SSD_GROUP_W = SSD_HEADS_PER_GROUP * SSD_HEAD_DIM

ATT_N_HEADS = 16
ATT_N_KV_HEADS = 4
ATT_REP = ATT_N_HEADS // ATT_N_KV_HEADS
ATT_HEAD_DIM = 64
ATT_Q_DIM = 1024
ATT_KV_DIM = 256
IDX_N_HEADS = 8
IDX_HEAD_DIM = 64
IDX_Q_DIM = 512
DSA_IN_DIM = 2120
DSA_IN_PAD = 2304
DSA_K_OFF = 1024
DSA_V_OFF = 1280
DSA_QI_OFF = 1536
DSA_KI_OFF = 2048
DSA_WI_LANE = 64
TOPK_MAX = 256

SC_WIDTH = 3

LANES = 128
SUBLANES = 8
VMEM_LIMIT = 56 * 1024 * 1024
NEG = -1e30
INT_MIN = -(2 ** 31)


def _cparams(*sem):
    return pltpu.CompilerParams(dimension_semantics=sem, vmem_limit_bytes=VMEM_LIMIT)


def _sigmoid(v):
    return 1.0 / (1.0 + jnp.exp(-v))


def _dot(a, b):
    return jnp.dot(a, b, preferred_element_type=F32)


def _dot_nt(a, b, precision=None):
    return lax.dot_general(a, b, (((1,), (1,)), ((), ())), precision=precision,
                           preferred_element_type=F32)


def _dot_tn(a, b):
    return lax.dot_general(a, b, (((0,), (0,)), ((), ())), preferred_element_type=F32)


def _rmsnorm_rows(x, g):
    ms = jnp.mean(x * x, axis=-1, keepdims=True)
    return x * lax.rsqrt(ms + EPS) * g


def _norm_matmul_kernel(x_ref, g_ref, w_ref, o_ref, xn_ref):
    @pl.when(pl.program_id(1) == 0)
    def _():
        xn_ref[...] = _rmsnorm_rows(x_ref[...], g_ref[...]).astype(BF16)

    o_ref[...] = _dot(xn_ref[...], w_ref[...])


def norm_matmul(x, g, w, tn):
    m, d = x.shape
    n = w.shape[1]
    tm = min(m, 1024)
    return pl.pallas_call(
        _norm_matmul_kernel,
        grid=(m // tm, n // tn),
        in_specs=[pl.BlockSpec((tm, d), lambda i, j: (i, 0)),
                  pl.BlockSpec((1, d), lambda i, j: (0, 0)),
                  pl.BlockSpec((d, tn), lambda i, j: (0, j))],
        out_specs=pl.BlockSpec((tm, tn), lambda i, j: (i, j)),
        out_shape=jax.ShapeDtypeStruct((m, n), F32),
        scratch_shapes=[pltpu.VMEM((tm, d), BF16)],
        compiler_params=_cparams("parallel", "arbitrary"),
        name="norm_matmul",
    )(x, g.reshape(1, d), w)


def _matmul_residual_kernel(y_ref, w_ref, h_ref, o_ref):
    o_ref[...] = h_ref[...] + _dot(y_ref[...], w_ref[...])


def matmul_residual(y, w, h):
    m, k = y.shape
    n = w.shape[1]
    tm = min(m, 512)
    return pl.pallas_call(
        _matmul_residual_kernel,
        grid=(m // tm,),
        in_specs=[pl.BlockSpec((tm, k), lambda i: (i, 0)),
                  pl.BlockSpec((k, n), lambda i: (0, 0)),
                  pl.BlockSpec((tm, n), lambda i: (i, 0))],
        out_specs=pl.BlockSpec((tm, n), lambda i: (i, 0)),
        out_shape=jax.ShapeDtypeStruct((m, n), F32),
        compiler_params=_cparams("parallel"),
        name="matmul_residual",
    )(y, w, h)


def _ffn_kernel(h_ref, g_ref, wg_ref, wu_ref, wd_ref, gf_ref, o_ref, *, final_norm):
    h = h_ref[...]
    xn = _rmsnorm_rows(h, g_ref[...]).astype(BF16)
    a = _dot(xn, wg_ref[...])
    b = _dot(xn, wu_ref[...])
    t = (a * _sigmoid(a) * b).astype(BF16)
    out = h + _dot(t, wd_ref[...])
    if final_norm:
        out = _rmsnorm_rows(out, gf_ref[...])
    o_ref[...] = out


def ffn(h, g, wg, wu, wd, gf, final_norm):
    m, d = h.shape
    f = wg.shape[1]
    tm = min(m, 512)
    resident = dict(pipeline_mode=pl.Buffered(1))
    return pl.pallas_call(
        functools.partial(_ffn_kernel, final_norm=final_norm),
        grid=(m // tm,),
        in_specs=[pl.BlockSpec((tm, d), lambda i: (i, 0)),
                  pl.BlockSpec((1, d), lambda i: (0, 0)),
                  pl.BlockSpec((d, f), lambda i: (0, 0), **resident),
                  pl.BlockSpec((d, f), lambda i: (0, 0), **resident),
                  pl.BlockSpec((f, d), lambda i: (0, 0), **resident),
                  pl.BlockSpec((1, d), lambda i: (0, 0))],
        out_specs=pl.BlockSpec((tm, d), lambda i: (i, 0)),
        out_shape=jax.ShapeDtypeStruct((m, d), F32),
        compiler_params=_cparams("parallel"),
        name="ffn",
    )(h, g.reshape(1, d), wg, wu, wd, gf.reshape(1, d))


def _expand_heads(v):
    r = v.shape[0]
    low_half = lax.broadcasted_iota(jnp.int32, (r, LANES), 1) < SSD_HEAD_DIM
    cols = []
    for p in range(SSD_N_HEADS // 2):
        lo = jnp.broadcast_to(v[:, 2 * p:2 * p + 1], (r, LANES))
        hi = jnp.broadcast_to(v[:, 2 * p + 1:2 * p + 2], (r, LANES))
        cols.append(jnp.where(low_half, lo, hi))
    return jnp.concatenate(cols, axis=1)


def _ssd_kernel(z_ref, x_ref, b_ref, c_ref, dt_ref, carry0_ref, s0_ref,
                convw_ref, convb_ref, dtb_ref, alog_ref, dskip_ref, normg_ref,
                y_ref, sout_ref, ext_ref, st_ref, yacc_ref, *, q, has_init):
    c = pl.program_id(1)
    halo = SUBLANES
    taps = SSD_CONV_W

    @pl.when(c == 0)
    def _():
        if has_init:
            ext_ref[0:halo, :] = carry0_ref[0]
            st_ref[...] = s0_ref[0]
        else:
            ext_ref[0:halo, :] = jnp.zeros((halo, SSD_CONV_DIM), F32)
            st_ref[...] = jnp.zeros(st_ref.shape, F32)

    ext_ref[halo:halo + q, 0:SSD_D_INNER] = x_ref[...]
    ext_ref[halo:halo + q, SSD_D_INNER:SSD_D_INNER + SSD_BC_DIM] = b_ref[...]
    ext_ref[halo:halo + q, SSD_D_INNER + SSD_BC_DIM:SSD_CONV_DIM] = c_ref[...]
    first = halo - (taps - 1)
    acc = jnp.broadcast_to(convb_ref[...], (q, SSD_CONV_DIM))
    for k in range(taps):
        acc = acc + ext_ref[first + k:first + k + q, :] * convw_ref[k:k + 1, :]
    xbc = acc * _sigmoid(acc)
    ext_ref[0:halo, :] = ext_ref[q:q + halo, :]
    xs = xbc[:, 0:SSD_D_INNER]
    xs_bf = xs.astype(BF16)
    bm_bf = xbc[:, SSD_D_INNER:SSD_D_INNER + SSD_BC_DIM].astype(BF16)
    cm_bf = xbc[:, SSD_D_INNER + SSD_BC_DIM:SSD_CONV_DIM].astype(BF16)

    v = dt_ref[...] + dtb_ref[...]
    dt = jnp.maximum(v, 0.0) + jnp.log1p(jnp.exp(-jnp.abs(v)))
    head_lane = lax.broadcasted_iota(jnp.int32, (1, LANES), 1) < SSD_N_HEADS
    a_neg = jnp.where(head_lane, -jnp.exp(alog_ref[...]), 0.0)
    da = dt * a_neg
    row = lax.broadcasted_iota(jnp.int32, (q, q), 0)
    col = lax.broadcasted_iota(jnp.int32, (q, q), 1)
    causal = row >= col
    tril = jnp.where(causal, 1.0, 0.0).astype(F32)
    cum = jnp.dot(tril, da, precision=lax.Precision.HIGHEST, preferred_element_type=F32)
    eye = jnp.where(lax.broadcasted_iota(jnp.int32, (LANES, LANES), 0)
                    == lax.broadcasted_iota(jnp.int32, (LANES, LANES), 1), 1.0, 0.0).astype(F32)
    cum_t = _dot_nt(eye, cum, precision=lax.Precision.HIGHEST)
    dt_t = _dot_nt(eye, dt, precision=lax.Precision.HIGHEST)
    cum_last = cum[q - 1:q, :]
    w_end_x = _expand_heads(jnp.exp(cum_last - cum) * dt)
    e_cum_x = _expand_heads(jnp.exp(cum))
    dec_x = _expand_heads(jnp.exp(cum_last))
    xw_bf = (xs * w_end_x).astype(BF16)

    for g in range(SSD_N_GROUPS):
        gl = slice(g * SSD_D_STATE, (g + 1) * SSD_D_STATE)
        gw = slice(g * SSD_GROUP_W, (g + 1) * SSD_GROUP_W)
        bg = bm_bf[:, gl]
        cg = cm_bf[:, gl]
        cb = _dot_nt(cg, bg)
        st_g = st_ref[g]
        y_inter = _dot(cg, st_g.astype(BF16)) * e_cum_x[:, gw]
        st_ref[g] = st_g * dec_x[:, gw] + _dot_tn(bg, xw_bf[:, gw])
        y_heads = []
        for r in range(SSD_HEADS_PER_GROUP):
            h = g * SSD_HEADS_PER_GROUP + r
            seg = cum[:, h:h + 1] - cum_t[h:h + 1, :]
            decay = jnp.exp(jnp.where(causal, seg, NEG))
            mix = (cb * decay * dt_t[h:h + 1, :]).astype(BF16)
            y_heads.append(_dot(mix, xs_bf[:, h * SSD_HEAD_DIM:(h + 1) * SSD_HEAD_DIM]))
        yacc_ref[:, gw] = jnp.concatenate(y_heads, axis=1) + y_inter

    zz = z_ref[...]
    y = (yacc_ref[...] + dskip_ref[...] * xs) * (zz * _sigmoid(zz))
    for g in range(SSD_N_GROUPS):
        gw = slice(g * SSD_GROUP_W, (g + 1) * SSD_GROUP_W)
        yg = y[:, gw]
        ms = jnp.mean(yg * yg, axis=-1, keepdims=True)
        y_ref[:, gw] = (yg * lax.rsqrt(ms + EPS) * normg_ref[:, gw]).astype(BF16)

    @pl.when(c == pl.num_programs(1) - 1)
    def _():
        sout_ref[0] = st_ref[...]


def ssd_core(proj, n_seq, length, carry0, s0, convw, convb, dtb, alog, dskip_x, normg, q):
    nc = length // q
    has_init = carry0 is not None
    if not has_init:
        carry0 = jnp.zeros((1, SUBLANES, SSD_CONV_DIM), F32)
        s0 = jnp.zeros((1, SSD_N_GROUPS, SSD_D_STATE, SSD_GROUP_W), F32)
        seq_map3 = lambda b, c: (0, 0, 0)
        seq_map4 = lambda b, c: (0, 0, 0, 0)
    else:
        seq_map3 = lambda b, c: (b, 0, 0)
        seq_map4 = lambda b, c: (b, 0, 0, 0)
    rows = lambda b, c: b * nc + c
    const2 = lambda b, c: (0, 0)
    x_blk = SSD_D_INNER // SSD_D_INNER
    b_blk = (2 * SSD_D_INNER) // SSD_BC_DIM
    c_blk = b_blk + 1
    dt_blk = (SSD_D_INNER + SSD_CONV_DIM) // LANES
    state_shape = (SSD_N_GROUPS, SSD_D_STATE, SSD_GROUP_W)
    return pl.pallas_call(
        functools.partial(_ssd_kernel, q=q, has_init=has_init),
        grid=(n_seq, nc),
        in_specs=[pl.BlockSpec((q, SSD_D_INNER), lambda b, c: (rows(b, c), 0)),
                  pl.BlockSpec((q, SSD_D_INNER), lambda b, c: (rows(b, c), x_blk)),
                  pl.BlockSpec((q, SSD_BC_DIM), lambda b, c: (rows(b, c), b_blk)),
                  pl.BlockSpec((q, SSD_BC_DIM), lambda b, c: (rows(b, c), c_blk)),
                  pl.BlockSpec((q, LANES), lambda b, c: (rows(b, c), dt_blk)),
                  pl.BlockSpec((1, SUBLANES, SSD_CONV_DIM), seq_map3),
                  pl.BlockSpec((1,) + state_shape, seq_map4),
                  pl.BlockSpec((SSD_CONV_W, SSD_CONV_DIM), const2),
                  pl.BlockSpec((1, SSD_CONV_DIM), const2),
                  pl.BlockSpec((1, LANES), const2),
                  pl.BlockSpec((1, LANES), const2),
                  pl.BlockSpec((1, SSD_D_INNER), const2),
                  pl.BlockSpec((1, SSD_D_INNER), const2)],
        out_specs=[pl.BlockSpec((q, SSD_D_INNER), lambda b, c: (rows(b, c), 0)),
                   pl.BlockSpec((1,) + state_shape, lambda b, c: (b, 0, 0, 0))],
        out_shape=[jax.ShapeDtypeStruct((n_seq * length, SSD_D_INNER), BF16),
                   jax.ShapeDtypeStruct((n_seq,) + state_shape, F32)],
        scratch_shapes=[pltpu.VMEM((q + SUBLANES, SSD_CONV_DIM), F32),
                        pltpu.VMEM(state_shape, F32),
                        pltpu.VMEM((q, SSD_D_INNER), F32)],
        compiler_params=_cparams("parallel", "arbitrary"),
        name="ssd_core",
    )(proj, proj, proj, proj, proj, carry0, s0, convw, convb, dtb, alog, dskip_x, normg)


def _pad_lanes(v):
    return jnp.zeros((1, LANES), F32).at[0, :v.shape[0]].set(v)


def ssd_mixer(h, n_seq, length, conv_state, ssm_state, g_mix, w):
    proj = norm_matmul(h, g_mix, w["w_in"], tn=1280)
    if conv_state is None:
        carry0 = s0 = None
    else:
        carry0 = jnp.concatenate(
            [jnp.zeros((n_seq, SUBLANES - (SSD_CONV_W - 1), SSD_CONV_DIM), F32), conv_state], axis=1)
        s0 = ssm_state.reshape(n_seq, SSD_N_GROUPS, SSD_HEADS_PER_GROUP, SSD_HEAD_DIM, SSD_D_STATE)
        s0 = s0.transpose(0, 1, 4, 2, 3).reshape(n_seq, SSD_N_GROUPS, SSD_D_STATE, SSD_GROUP_W)
    y, st = ssd_core(proj, n_seq, length, carry0, s0, w["conv_w"], w["conv_b"], w["dt_bias"],
                     w["a_log"], w["d_skip_x"], w["norm_g"], q=CHUNK)
    h = matmul_residual(y, w["w_out"], h)
    new_conv = proj.reshape(n_seq, length, SSD_IN_PAD)[:, length - (SSD_CONV_W - 1):,
                                                        SSD_D_INNER:SSD_D_INNER + SSD_CONV_DIM]
    new_ssm = st.reshape(n_seq, SSD_N_GROUPS, SSD_D_STATE, SSD_HEADS_PER_GROUP, SSD_HEAD_DIM)
    new_ssm = new_ssm.transpose(0, 1, 3, 4, 2).reshape(n_seq, SSD_N_HEADS, SSD_HEAD_DIM, SSD_D_STATE)
    return h, new_conv, new_ssm


def prep_ssd_weights(w_in, conv_w, conv_b, dt_bias, a_log, d_skip, norm_g, w_out):
    w_in_p = jnp.zeros((D_MODEL, SSD_IN_PAD), BF16).at[:, :SSD_IN_DIM].set(w_in.astype(BF16))
    return dict(w_in=w_in_p, conv_w=conv_w, conv_b=conv_b.reshape(1, SSD_CONV_DIM),
                dt_bias=_pad_lanes(dt_bias), a_log=_pad_lanes(a_log),
                d_skip_x=jnp.repeat(d_skip, SSD_HEAD_DIM).reshape(1, SSD_D_INNER),
                norm_g=norm_g.reshape(1, SSD_D_INNER), w_out=w_out.astype(BF16))


def _dsa_kernel(q_ref, qi_ref, kw_ref, k_ref, v_ref, ki_ref, o_ref, key_ref, bias_ref,
                *, tq, s_pad, n_keys, past, topk):
    i = pl.program_id(1)
    kpos = lax.broadcasted_iota(jnp.int32, (tq, s_pad), 1)
    qpos = past + i * tq + lax.broadcasted_iota(jnp.int32, (tq, s_pad), 0)
    kchunk = jnp.where(kpos < n_keys, kpos >> 6, jnp.int32(2 ** 30))
    admissible = kchunk <= (qpos >> 6)

    qi = (qi_ref[...] * (IDX_HEAD_DIM ** -0.5)).astype(BF16)
    wi = kw_ref[:, DSA_WI_LANE:DSA_WI_LANE + IDX_N_HEADS] * (IDX_N_HEADS ** -0.5)
    ki = ki_ref[0]
    score = jnp.zeros((tq, s_pad), F32)
    for h in range(IDX_N_HEADS):
        d = _dot_nt(qi[:, h * IDX_HEAD_DIM:(h + 1) * IDX_HEAD_DIM], ki)
        score = score + wi[:, h:h + 1] * jnp.maximum(d, 0.0)
    score = jnp.where(admissible, score + 0.0, -jnp.inf)

    bits = pltpu.bitcast(score, jnp.int32)
    key_ref[...] = jnp.where(bits < 0, bits ^ jnp.int32(0x7FFFFFFF), bits)
    kf = jnp.float32(topk)

    def bisect(step, t):
        cand = t + jnp.left_shift(jnp.int32(1), 31 - step)
        cnt = jnp.sum(jnp.where(key_ref[...] >= cand, 1.0, 0.0), axis=1, keepdims=True)
        return jnp.where(cnt >= kf, cand, t)

    thr = lax.fori_loop(0, 32, bisect, jnp.full((tq, 1), INT_MIN, jnp.int32))

    key = key_ref[...]
    above = key > thr
    tie = key == thr
    need = kf - jnp.sum(jnp.where(above, 1.0, 0.0), axis=1, keepdims=True)
    upper = jnp.where(lax.broadcasted_iota(jnp.int32, (LANES, LANES), 0)
                      <= lax.broadcasted_iota(jnp.int32, (LANES, LANES), 1), 1.0, 0.0).astype(BF16)
    seen = jnp.zeros((tq, 1), F32)
    for blk in range(s_pad // LANES):
        sl = slice(blk * LANES, (blk + 1) * LANES)
        tie_b = tie[:, sl]
        rank = _dot(jnp.where(tie_b, 1.0, 0.0).astype(BF16), upper) + seen
        keep = jnp.where(above[:, sl], 0.0, jnp.where(tie_b, jnp.where(rank <= need, 0.0, NEG), NEG))
        bias_ref[:, sl] = jnp.where(admissible[:, sl], keep, NEG)
        seen = rank[:, LANES - 1:LANES]

    qq = q_ref[...].astype(BF16)
    bias = bias_ref[...]
    for g in range(ATT_N_KV_HEADS):
        heads = [qq[:, (g * ATT_REP + r) * ATT_HEAD_DIM:(g * ATT_REP + r + 1) * ATT_HEAD_DIM]
                 for r in range(ATT_REP)]
        qg = jnp.concatenate(heads, axis=0)
        kg = k_ref[0, :, g * ATT_HEAD_DIM:(g + 1) * ATT_HEAD_DIM]
        vg = v_ref[0, :, g * ATT_HEAD_DIM:(g + 1) * ATT_HEAD_DIM]
        logits = _dot_nt(qg, kg) * (ATT_HEAD_DIM ** -0.5)
        logits = (logits.reshape(ATT_REP, tq, s_pad) + bias[None]).reshape(ATT_REP * tq, s_pad)
        m = jnp.max(logits, axis=1, keepdims=True)
        p = jnp.exp(logits - m)
        denom = jnp.sum(p, axis=1, keepdims=True)
        og = _dot(p.astype(BF16), vg) / denom
        for r in range(ATT_REP):
            hh = g * ATT_REP + r
            o_ref[:, hh * ATT_HEAD_DIM:(hh + 1) * ATT_HEAD_DIM] = og[r * tq:(r + 1) * tq].astype(BF16)


def dsa_core(proj, n_seq, length, k_all, v_all, ki_all, n_keys, past, tq):
    s_pad = k_all.shape[1]
    nq = length // tq
    topk = min(TOPK_MAX, n_keys // 4)
    rows = lambda b, i: b * nq + i
    return pl.pallas_call(
        functools.partial(_dsa_kernel, tq=tq, s_pad=s_pad, n_keys=n_keys, past=past, topk=topk),
        grid=(n_seq, nq),
        in_specs=[pl.BlockSpec((tq, ATT_Q_DIM), lambda b, i: (rows(b, i), 0)),
                  pl.BlockSpec((tq, IDX_Q_DIM), lambda b, i: (rows(b, i), DSA_QI_OFF // IDX_Q_DIM)),
                  pl.BlockSpec((tq, LANES), lambda b, i: (rows(b, i), DSA_KI_OFF // LANES)),
                  pl.BlockSpec((1, s_pad, ATT_KV_DIM), lambda b, i: (b, 0, 0)),
                  pl.BlockSpec((1, s_pad, ATT_KV_DIM), lambda b, i: (b, 0, 0)),
                  pl.BlockSpec((1, s_pad, IDX_HEAD_DIM), lambda b, i: (b, 0, 0))],
        out_specs=pl.BlockSpec((tq, ATT_Q_DIM), lambda b, i: (rows(b, i), 0)),
        out_shape=jax.ShapeDtypeStruct((n_seq * length, ATT_Q_DIM), BF16),
        scratch_shapes=[pltpu.VMEM((tq, s_pad), jnp.int32),
                        pltpu.VMEM((tq, s_pad), F32)],
        compiler_params=_cparams("parallel", "arbitrary"),
        name="dsa_core",
    )(proj, proj, proj, k_all, v_all, ki_all)


def dsa_mixer(h, n_seq, length, cache, g_mix, w):
    proj = norm_matmul(h, g_mix, w["w_in"], tn=1152)
    p3 = proj.reshape(n_seq, length, DSA_IN_PAD)
    k_new = p3[:, :, DSA_K_OFF:DSA_K_OFF + ATT_KV_DIM]
    v_new = p3[:, :, DSA_V_OFF:DSA_V_OFF + ATT_KV_DIM]
    ki_new = p3[:, :, DSA_KI_OFF:DSA_KI_OFF + IDX_HEAD_DIM]
    if cache is None:
        past = 0
        parts = ([k_new], [v_new], [ki_new])
    else:
        cache_k, cache_v, cache_ki = cache
        past = cache_k.shape[1]
        parts = ([cache_k.reshape(n_seq, past, ATT_KV_DIM), k_new],
                 [cache_v.reshape(n_seq, past, ATT_KV_DIM), v_new],
                 [cache_ki, ki_new])
    n_keys = past + length
    s_pad = -(-n_keys // LANES) * LANES

    def keys(ps):
        ps = [p.astype(BF16) for p in ps]
        if s_pad > n_keys:
            ps.append(jnp.zeros((n_seq, s_pad - n_keys, ps[0].shape[2]), BF16))
        return ps[0] if len(ps) == 1 else jnp.concatenate(ps, axis=1)

    tq = min(length, 128)
    o = dsa_core(proj, n_seq, length, keys(parts[0]), keys(parts[1]), keys(parts[2]), n_keys, past, tq)
    h = matmul_residual(o, w["w_out"], h)
    return (h, k_new.reshape(n_seq, length, ATT_N_KV_HEADS, ATT_HEAD_DIM),
            v_new.reshape(n_seq, length, ATT_N_KV_HEADS, ATT_HEAD_DIM), ki_new)


def _shortconv_kernel(gb_ref, gc_ref, u_ref, h_ref, carry0_ref, cw_ref, wout_ref,
                      o_ref, sout_ref, ext_ref, *, t, has_init):
    i = pl.program_id(1)
    halo = SUBLANES

    @pl.when(i == 0)
    def _():
        if has_init:
            ext_ref[0:halo, :] = carry0_ref[0]
        else:
            ext_ref[0:halo, :] = jnp.zeros((halo, D_MODEL), F32)

    ext_ref[halo:halo + t, :] = gc_ref[...] * u_ref[...]
    first = halo - (SC_WIDTH - 1)
    y = ext_ref[first:first + t, :] * cw_ref[0:1, :]
    for k in range(1, SC_WIDTH):
        y = y + ext_ref[first + k:first + k + t, :] * cw_ref[k:k + 1, :]
    ext_ref[0:halo, :] = ext_ref[t:t + halo, :]
    o_ref[...] = h_ref[...] + _dot((gb_ref[...] * y).astype(BF16), wout_ref[...])

    @pl.when(i == pl.num_programs(1) - 1)
    def _():
        sout_ref[0] = ext_ref[0:halo, :]


def shortconv_mixer(h, n_seq, length, conv_state, g_mix, w):
    proj = norm_matmul(h, g_mix, w["w_in"], tn=1536)
    t = min(length, 512)
    nt = length // t
    has_init = conv_state is not None
    if has_init:
        carry0 = jnp.concatenate(
            [jnp.zeros((n_seq, SUBLANES - (SC_WIDTH - 1), D_MODEL), F32), conv_state], axis=1)
        seq_map = lambda b, i: (b, 0, 0)
    else:
        carry0 = jnp.zeros((1, SUBLANES, D_MODEL), F32)
        seq_map = lambda b, i: (0, 0, 0)
    rows = lambda b, i: b * nt + i
    h_new, st = pl.pallas_call(
        functools.partial(_shortconv_kernel, t=t, has_init=has_init),
        grid=(n_seq, nt),
        in_specs=[pl.BlockSpec((t, D_MODEL), lambda b, i: (rows(b, i), 0)),
                  pl.BlockSpec((t, D_MODEL), lambda b, i: (rows(b, i), 1)),
                  pl.BlockSpec((t, D_MODEL), lambda b, i: (rows(b, i), 2)),
                  pl.BlockSpec((t, D_MODEL), lambda b, i: (rows(b, i), 0)),
                  pl.BlockSpec((1, SUBLANES, D_MODEL), seq_map),
                  pl.BlockSpec((SC_WIDTH, D_MODEL), lambda b, i: (0, 0)),
                  pl.BlockSpec((D_MODEL, D_MODEL), lambda b, i: (0, 0))],
        out_specs=[pl.BlockSpec((t, D_MODEL), lambda b, i: (rows(b, i), 0)),
                   pl.BlockSpec((1, SUBLANES, D_MODEL), lambda b, i: (b, 0, 0))],
        out_shape=[jax.ShapeDtypeStruct((n_seq * length, D_MODEL), F32),
                   jax.ShapeDtypeStruct((n_seq, SUBLANES, D_MODEL), F32)],
        scratch_shapes=[pltpu.VMEM((t + SUBLANES, D_MODEL), F32)],
        compiler_params=_cparams("parallel", "arbitrary"),
        name="shortconv_core",
    )(proj, proj, proj, h, carry0, w["conv_w"], w["w_out"])
    return h_new, st[:, SUBLANES - (SC_WIDTH - 1):, :]


def _trunk(x, states, weights, norm_mix_g, norm_ffn_g, norm_final_g, ffn_w):
    n_seq, length, _ = x.shape
    h = x.reshape(n_seq * length, D_MODEL)
    depth = len(weights)
    new_states = []
    for i in range(depth):
        kind = i % 3
        st = states[i]
        if kind == 0:
            conv_state, ssm_state = st if st is not None else (None, None)
            h, new_conv, new_ssm = ssd_mixer(h, n_seq, length, conv_state, ssm_state,
                                             norm_mix_g[i], weights[i])
            new_states.append((new_conv, new_ssm))
        elif kind == 1:
            h, k_new, v_new, ki_new = dsa_mixer(h, n_seq, length, st, norm_mix_g[i], weights[i])
            new_states.append((k_new, v_new, ki_new))
        else:
            h, new_conv = shortconv_mixer(h, n_seq, length, st[0] if st is not None else None,
                                          norm_mix_g[i], weights[i])
            new_states.append((new_conv,))
        wg, wu, wd = ffn_w[i]
        h = ffn(h, norm_ffn_g[i], wg, wu, wd, norm_final_g, final_norm=(i == depth - 1))
    return h.reshape(n_seq, length, D_MODEL), new_states


def kernel(x_prompt, x_sample, state_l0_conv, state_l0_ssm, cache_l1_k, cache_l1_v, cache_l1_kidx, state_l2_conv, state_l3_conv, state_l3_ssm, norm_mix_g, norm_ffn_g, norm_final_g, ffn_w_gate, ffn_w_up, ffn_w_down, l0_w_in, l0_conv_w, l0_conv_b, l0_dt_bias, l0_a_log, l0_d_skip, l0_norm_g, l0_w_out, l1_w_in, l1_w_out, l2_w_in, l2_conv_w, l2_w_out, l3_w_in, l3_conv_w, l3_conv_b, l3_dt_bias, l3_a_log, l3_d_skip, l3_norm_g, l3_w_out):
    depth = norm_mix_g.shape[0]
    weights = [
        prep_ssd_weights(l0_w_in, l0_conv_w, l0_conv_b, l0_dt_bias, l0_a_log, l0_d_skip, l0_norm_g, l0_w_out),
        dict(w_in=jnp.zeros((D_MODEL, DSA_IN_PAD), BF16).at[:, :DSA_IN_DIM].set(l1_w_in.astype(BF16)),
             w_out=l1_w_out.astype(BF16)),
        dict(w_in=l2_w_in.astype(BF16), conv_w=l2_conv_w, w_out=l2_w_out.astype(BF16)),
        prep_ssd_weights(l3_w_in, l3_conv_w, l3_conv_b, l3_dt_bias, l3_a_log, l3_d_skip, l3_norm_g, l3_w_out),
    ]
    ffn_w = [(ffn_w_gate[i].astype(BF16), ffn_w_up[i].astype(BF16), ffn_w_down[i].astype(BF16))
             for i in range(depth)]
    sample_states = [(state_l0_conv, state_l0_ssm), (cache_l1_k, cache_l1_v, cache_l1_kidx),
                     (state_l2_conv,), (state_l3_conv, state_l3_ssm)]
    prompt_states = [None] * depth
    y_p, new_p = _trunk(x_prompt, prompt_states, weights, norm_mix_g, norm_ffn_g, norm_final_g, ffn_w)
    y_s, new_s = _trunk(x_sample, sample_states, weights, norm_mix_g, norm_ffn_g, norm_final_g, ffn_w)
    (p_l0_conv, p_l0_ssm), (p_l1_k, p_l1_v, p_l1_kidx), (p_l2_conv,), (p_l3_conv, p_l3_ssm) = new_p
    (s_l0_conv, s_l0_ssm), (s_l1_k, s_l1_v, s_l1_kidx), (s_l2_conv,), (s_l3_conv, s_l3_ssm) = new_s
    return (y_p, y_s,
            p_l0_conv, p_l0_ssm, s_l0_conv, s_l0_ssm,
            p_l1_k, p_l1_v, p_l1_kidx, s_l1_k, s_l1_v, s_l1_kidx,
            p_l2_conv, s_l2_conv,
            p_l3_conv, p_l3_ssm, s_l3_conv, s_l3_ssm)
```

```python
import functools

import jax
import jax.numpy as jnp
from jax import lax
from jax.experimental import pallas as pl
from jax.experimental.pallas import tpu as pltpu

F32 = jnp.float32
BF16 = jnp.bfloat16

EPS = 1e-6
CHUNK = 64

D_MODEL = 1024
FFN_HIDDEN = 2816

SSD_D_INNER = 2048
SSD_HEAD_DIM = 64
SSD_N_HEADS = 32
SSD_N_GROUPS = 8
SSD_HEADS_PER_GROUP = 4
SSD_D_STATE = 128
SSD_CONV_W = 4
SSD_BC_DIM = SSD_N_GROUPS * SSD_D_STATE
SSD_CONV_DIM = SSD_D_INNER + 2 * SSD_BC_DIM
SSD_IN_DIM = SSD_D_INNER + SSD_CONV_DIM + SSD_N_HEADS
SSD_IN_PAD = 6400
SSD_GROUP_W = SSD_HEADS_PER_GROUP * SSD_HEAD_DIM

ATT_N_HEADS = 16
ATT_N_KV_HEADS = 4
ATT_REP = ATT_N_HEADS // ATT_N_KV_HEADS
ATT_HEAD_DIM = 64
ATT_Q_DIM = 1024
ATT_KV_DIM = 256
IDX_N_HEADS = 8
IDX_HEAD_DIM = 64
IDX_Q_DIM = 512
DSA_IN_DIM = 2120
DSA_IN_PAD = 2304
DSA_K_OFF = 1024
DSA_V_OFF = 1280
DSA_QI_OFF = 1536
DSA_KI_OFF = 2048
DSA_WI_LANE = 64
TOPK_MAX = 256

SC_WIDTH = 3

LANES = 128
SUBLANES = 8
VMEM_LIMIT = 56 * 1024 * 1024
NEG = -1e30
INT_MIN = -(2 ** 31)


def _cparams(*sem):
    return pltpu.CompilerParams(dimension_semantics=sem, vmem_limit_bytes=VMEM_LIMIT)


def _sigmoid(v):
    return 1.0 / (1.0 + jnp.exp(-v))


def _dot(a, b):
    return jnp.dot(a, b, preferred_element_type=F32)


def _dot_nt(a, b, precision=None):
    return lax.dot_general(a, b, (((1,), (1,)), ((), ())), precision=precision,
                           preferred_element_type=F32)


def _dot_tn(a, b):
    return lax.dot_general(a, b, (((0,), (0,)), ((), ())), preferred_element_type=F32)


def _rmsnorm_rows(x, g):
    ms = jnp.mean(x * x, axis=-1, keepdims=True)
    return x * lax.rsqrt(ms + EPS) * g


def _norm_matmul_kernel(x_ref, g_ref, w_ref, o_ref, xn_ref):
    @pl.when(pl.program_id(1) == 0)
    def _():
        xn_ref[...] = _rmsnorm_rows(x_ref[...], g_ref[...]).astype(BF16)

    o_ref[...] = _dot(xn_ref[...], w_ref[...])


def norm_matmul(x, g, w, tn):
    m, d = x.shape
    n = w.shape[1]
    tm = min(m, 1024)
    return pl.pallas_call(
        _norm_matmul_kernel,
        grid=(m // tm, n // tn),
        in_specs=[pl.BlockSpec((tm, d), lambda i, j: (i, 0)),
                  pl.BlockSpec((1, d), lambda i, j: (0, 0)),
                  pl.BlockSpec((d, tn), lambda i, j: (0, j))],
        out_specs=pl.BlockSpec((tm, tn), lambda i, j: (i, j)),
        out_shape=jax.ShapeDtypeStruct((m, n), F32),
        scratch_shapes=[pltpu.VMEM((tm, d), BF16)],
        compiler_params=_cparams("parallel", "arbitrary"),
        name="norm_matmul",
    )(x, g.reshape(1, d), w)


def _matmul_residual_kernel(y_ref, w_ref, h_ref, o_ref):
    o_ref[...] = h_ref[...] + _dot(y_ref[...], w_ref[...])


def matmul_residual(y, w, h):
    m, k = y.shape
    n = w.shape[1]
    tm = min(m, 512)
    return pl.pallas_call(
        _matmul_residual_kernel,
        grid=(m // tm,),
        in_specs=[pl.BlockSpec((tm, k), lambda i: (i, 0)),
                  pl.BlockSpec((k, n), lambda i: (0, 0)),
                  pl.BlockSpec((tm, n), lambda i: (i, 0))],
        out_specs=pl.BlockSpec((tm, n), lambda i: (i, 0)),
        out_shape=jax.ShapeDtypeStruct((m, n), F32),
        compiler_params=_cparams("parallel"),
        name="matmul_residual",
    )(y, w, h)


def _ffn_kernel(h_ref, g_ref, wg_ref, wu_ref, wd_ref, gf_ref, o_ref, *, final_norm):
    h = h_ref[...]
    xn = _rmsnorm_rows(h, g_ref[...]).astype(BF16)
    a = _dot(xn, wg_ref[...])
    b = _dot(xn, wu_ref[...])
    t = (a * _sigmoid(a) * b).astype(BF16)
    out = h + _dot(t, wd_ref[...])
    if final_norm:
        out = _rmsnorm_rows(out, gf_ref[...])
    o_ref[...] = out


def ffn(h, g, wg, wu, wd, gf, final_norm):
    m, d = h.shape
    f = wg.shape[1]
    tm = min(m, 512)
    resident = dict(pipeline_mode=pl.Buffered(1))
    return pl.pallas_call(
        functools.partial(_ffn_kernel, final_norm=final_norm),
        grid=(m // tm,),
        in_specs=[pl.BlockSpec((tm, d), lambda i: (i, 0)),
                  pl.BlockSpec((1, d), lambda i: (0, 0)),
                  pl.BlockSpec((d, f), lambda i: (0, 0), **resident),
                  pl.BlockSpec((d, f), lambda i: (0, 0), **resident),
                  pl.BlockSpec((f, d), lambda i: (0, 0), **resident),
                  pl.BlockSpec((1, d), lambda i: (0, 0))],
        out_specs=pl.BlockSpec((tm, d), lambda i: (i, 0)),
        out_shape=jax.ShapeDtypeStruct((m, d), F32),
        compiler_params=_cparams("parallel"),
        name="ffn",
    )(h, g.reshape(1, d), wg, wu, wd, gf.reshape(1, d))


def _expand_heads(v):
    r = v.shape[0]
    low_half = lax.broadcasted_iota(jnp.int32, (r, LANES), 1) < SSD_HEAD_DIM
    cols = []
    for p in range(SSD_N_HEADS // 2):
        lo = jnp.broadcast_to(v[:, 2 * p:2 * p + 1], (r, LANES))
        hi = jnp.broadcast_to(v[:, 2 * p + 1:2 * p + 2], (r, LANES))
        cols.append(jnp.where(low_half, lo, hi))
    return jnp.concatenate(cols, axis=1)


def _ssd_kernel(z_ref, x_ref, b_ref, c_ref, dt_ref, carry0_ref, s0_ref,
                convw_ref, convb_ref, dtb_ref, alog_ref, dskip_ref, normg_ref,
                y_ref, sout_ref, ext_ref, st_ref, yacc_ref, *, q, has_init):
    c = pl.program_id(1)
    halo = SUBLANES
    taps = SSD_CONV_W

    @pl.when(c == 0)
    def _():
        if has_init:
            ext_ref[0:halo, :] = carry0_ref[0]
            for g in range(SSD_N_GROUPS):
                st_ref[g] = s0_ref[0, g].T
        else:
            ext_ref[0:halo, :] = jnp.zeros((halo, SSD_CONV_DIM), F32)
            st_ref[...] = jnp.zeros(st_ref.shape, F32)

    ext_ref[halo:halo + q, 0:SSD_D_INNER] = x_ref[...]
    ext_ref[halo:halo + q, SSD_D_INNER:SSD_D_INNER + SSD_BC_DIM] = b_ref[...]
    ext_ref[halo:halo + q, SSD_D_INNER + SSD_BC_DIM:SSD_CONV_DIM] = c_ref[...]
    first = halo - (taps - 1)
    acc = jnp.broadcast_to(convb_ref[...], (q, SSD_CONV_DIM))
    for k in range(taps):
        acc = acc + ext_ref[first + k:first + k + q, :] * convw_ref[k:k + 1, :]
    xbc = acc * _sigmoid(acc)
    ext_ref[0:halo, :] = ext_ref[q:q + halo, :]
    xs = xbc[:, 0:SSD_D_INNER]
    xs_bf = xs.astype(BF16)
    bm_bf = xbc[:, SSD_D_INNER:SSD_D_INNER + SSD_BC_DIM].astype(BF16)
    cm_bf = xbc[:, SSD_D_INNER + SSD_BC_DIM:SSD_CONV_DIM].astype(BF16)

    v = dt_ref[...] + dtb_ref[...]
    dt = jnp.maximum(v, 0.0) + jnp.log1p(jnp.exp(-jnp.abs(v)))
    head_lane = lax.broadcasted_iota(jnp.int32, (1, LANES), 1) < SSD_N_HEADS
    a_neg = jnp.where(head_lane, -jnp.exp(alog_ref[...]), 0.0)
    da = dt * a_neg
    row = lax.broadcasted_iota(jnp.int32, (q, q), 0)
    col = lax.broadcasted_iota(jnp.int32, (q, q), 1)
    causal = row >= col
    tril = jnp.where(causal, 1.0, 0.0).astype(F32)
    cum = jnp.dot(tril, da, precision=lax.Precision.HIGHEST, preferred_element_type=F32)
    eye = jnp.where(lax.broadcasted_iota(jnp.int32, (LANES, LANES), 0)
                    == lax.broadcasted_iota(jnp.int32, (LANES, LANES), 1), 1.0, 0.0).astype(F32)
    cum_t = _dot_nt(eye, cum, precision=lax.Precision.HIGHEST)
    dt_t = _dot_nt(eye, dt, precision=lax.Precision.HIGHEST)
    cum_last = cum[q - 1:q, :]
    w_end_x = _expand_heads(jnp.exp(cum_last - cum) * dt)
    e_cum_x = _expand_heads(jnp.exp(cum))
    dec_x = _expand_heads(jnp.exp(cum_last))
    xw_bf = (xs * w_end_x).astype(BF16)

    for g in range(SSD_N_GROUPS):
        gl = slice(g * SSD_D_STATE, (g + 1) * SSD_D_STATE)
        gw = slice(g * SSD_GROUP_W, (g + 1) * SSD_GROUP_W)
        bg = bm_bf[:, gl]
        cg = cm_bf[:, gl]
        cb = _dot_nt(cg, bg)
        st_g = st_ref[g]
        y_inter = _dot(cg, st_g.astype(BF16)) * e_cum_x[:, gw]
        st_ref[g] = st_g * dec_x[:, gw] + _dot_tn(bg, xw_bf[:, gw])
        y_heads = []
        for r in range(SSD_HEADS_PER_GROUP):
            h = g * SSD_HEADS_PER_GROUP + r
            seg = cum[:, h:h + 1] - cum_t[h:h + 1, :]
            decay = jnp.exp(jnp.where(causal, seg, NEG))
            mix = (cb * decay * dt_t[h:h + 1, :]).astype(BF16)
            y_heads.append(_dot(mix, xs_bf[:, h * SSD_HEAD_DIM:(h + 1) * SSD_HEAD_DIM]))
        yacc_ref[:, gw] = jnp.concatenate(y_heads, axis=1) + y_inter

    zz = z_ref[...]
    y = (yacc_ref[...] + dskip_ref[...] * xs) * (zz * _sigmoid(zz))
    for g in range(SSD_N_GROUPS):
        gw = slice(g * SSD_GROUP_W, (g + 1) * SSD_GROUP_W)
        yg = y[:, gw]
        ms = jnp.mean(yg * yg, axis=-1, keepdims=True)
        y_ref[:, gw] = (yg * lax.rsqrt(ms + EPS) * normg_ref[:, gw]).astype(BF16)

    @pl.when(c == pl.num_programs(1) - 1)
    def _():
        for g in range(SSD_N_GROUPS):
            sout_ref[0, g] = st_ref[g].T


def ssd_core(proj, n_seq, length, carry0, s0, convw, convb, dtb, alog, dskip_x, normg, q):
    nc = length // q
    has_init = carry0 is not None
    if not has_init:
        carry0 = jnp.zeros((1, SUBLANES, SSD_CONV_DIM), F32)
        s0 = jnp.zeros((1, SSD_N_GROUPS, SSD_GROUP_W, SSD_D_STATE), F32)
        seq_map3 = lambda b, c: (0, 0, 0)
        seq_map4 = lambda b, c: (0, 0, 0, 0)
    else:
        seq_map3 = lambda b, c: (b, 0, 0)
        seq_map4 = lambda b, c: (b, 0, 0, 0)
    rows = lambda b, c: b * nc + c
    const2 = lambda b, c: (0, 0)
    x_blk = SSD_D_INNER // SSD_D_INNER
    b_blk = (2 * SSD_D_INNER) // SSD_BC_DIM
    c_blk = b_blk + 1
    dt_blk = (SSD_D_INNER + SSD_CONV_DIM) // LANES
    state_shape = (SSD_N_GROUPS, SSD_D_STATE, SSD_GROUP_W)
    state_io_shape = (SSD_N_GROUPS, SSD_GROUP_W, SSD_D_STATE)
    return pl.pallas_call(
        functools.partial(_ssd_kernel, q=q, has_init=has_init),
        grid=(n_seq, nc),
        in_specs=[pl.BlockSpec((q, SSD_D_INNER), lambda b, c: (rows(b, c), 0)),
                  pl.BlockSpec((q, SSD_D_INNER), lambda b, c: (rows(b, c), x_blk)),
                  pl.BlockSpec((q, SSD_BC_DIM), lambda b, c: (rows(b, c), b_blk)),
                  pl.BlockSpec((q, SSD_BC_DIM), lambda b, c: (rows(b, c), c_blk)),
                  pl.BlockSpec((q, LANES), lambda b, c: (rows(b, c), dt_blk)),
                  pl.BlockSpec((1, SUBLANES, SSD_CONV_DIM), seq_map3),
                  pl.BlockSpec((1,) + state_io_shape, seq_map4),
                  pl.BlockSpec((SSD_CONV_W, SSD_CONV_DIM), const2),
                  pl.BlockSpec((1, SSD_CONV_DIM), const2),
                  pl.BlockSpec((1, LANES), const2),
                  pl.BlockSpec((1, LANES), const2),
                  pl.BlockSpec((1, SSD_D_INNER), const2),
                  pl.BlockSpec((1, SSD_D_INNER), const2)],
        out_specs=[pl.BlockSpec((q, SSD_D_INNER), lambda b, c: (rows(b, c), 0)),
                   pl.BlockSpec((1,) + state_io_shape, lambda b, c: (b, 0, 0, 0))],
        out_shape=[jax.ShapeDtypeStruct((n_seq * length, SSD_D_INNER), BF16),
                   jax.ShapeDtypeStruct((n_seq,) + state_io_shape, F32)],
        scratch_shapes=[pltpu.VMEM((q + SUBLANES, SSD_CONV_DIM), F32),
                        pltpu.VMEM(state_shape, F32),
                        pltpu.VMEM((q, SSD_D_INNER), F32)],
        compiler_params=_cparams("parallel", "arbitrary"),
        name="ssd_core",
    )(proj, proj, proj, proj, proj, carry0, s0, convw, convb, dtb, alog, dskip_x, normg)


def _pad_lanes(v):
    return jnp.zeros((1, LANES), F32).at[0, :v.shape[0]].set(v)


def ssd_mixer(h, n_seq, length, conv_state, ssm_state, g_mix, w):
    proj = norm_matmul(h, g_mix, w["w_in"], tn=1280)
    if conv_state is None:
        carry0 = s0 = None
    else:
        carry0 = jnp.concatenate(
            [jnp.zeros((n_seq, SUBLANES - (SSD_CONV_W - 1), SSD_CONV_DIM), F32), conv_state], axis=1)
        s0 = ssm_state.reshape(n_seq, SSD_N_GROUPS, SSD_GROUP_W, SSD_D_STATE)
    y, st = ssd_core(proj, n_seq, length, carry0, s0, w["conv_w"], w["conv_b"], w["dt_bias"],
                     w["a_log"], w["d_skip_x"], w["norm_g"], q=CHUNK)
    h = matmul_residual(y, w["w_out"], h)
    new_conv = proj.reshape(n_seq, length, SSD_IN_PAD)[:, length - (SSD_CONV_W - 1):,
                                                        SSD_D_INNER:SSD_D_INNER + SSD_CONV_DIM]
    return h, new_conv, st.reshape(n_seq, SSD_N_HEADS, SSD_HEAD_DIM, SSD_D_STATE)


def prep_ssd_weights(w_in, conv_w, conv_b, dt_bias, a_log, d_skip, norm_g, w_out):
    w_in_p = jnp.zeros((D_MODEL, SSD_IN_PAD), BF16).at[:, :SSD_IN_DIM].set(w_in.astype(BF16))
    return dict(w_in=w_in_p, conv_w=conv_w, conv_b=conv_b.reshape(1, SSD_CONV_DIM),
                dt_bias=_pad_lanes(dt_bias), a_log=_pad_lanes(a_log),
                d_skip_x=jnp.repeat(d_skip, SSD_HEAD_DIM).reshape(1, SSD_D_INNER),
                norm_g=norm_g.reshape(1, SSD_D_INNER), w_out=w_out.astype(BF16))


def _dsa_in_proj_kernel(x_ref, g_ref, w_ref, q_ref, qi_ref, kw_ref, k_ref, v_ref, ki_ref,
                        kb_ref, vb_ref, kib_ref):
    proj = _dot(_rmsnorm_rows(x_ref[...], g_ref[...]).astype(BF16), w_ref[...])
    q_ref[...] = (proj[:, 0:ATT_Q_DIM] * (ATT_HEAD_DIM ** -0.5)).astype(BF16)
    qi_ref[...] = (proj[:, DSA_QI_OFF:DSA_QI_OFF + IDX_Q_DIM] * (IDX_HEAD_DIM ** -0.5)).astype(BF16)
    kw_ref[...] = proj[:, DSA_KI_OFF:DSA_KI_OFF + LANES]
    k = proj[:, DSA_K_OFF:DSA_K_OFF + ATT_KV_DIM]
    v = proj[:, DSA_V_OFF:DSA_V_OFF + ATT_KV_DIM]
    ki = proj[:, DSA_KI_OFF:DSA_KI_OFF + IDX_HEAD_DIM]
    k_ref[...] = k
    v_ref[...] = v
    ki_ref[...] = ki
    kb_ref[...] = k.astype(BF16)
    vb_ref[...] = v.astype(BF16)
    kib_ref[...] = ki.astype(BF16)


def dsa_in_proj(h, g, w):
    m, d = h.shape
    tm = min(m, 512)
    row = lambda i: (i, 0)
    widths = [(ATT_Q_DIM, BF16), (IDX_Q_DIM, BF16), (LANES, F32),
              (ATT_KV_DIM, F32), (ATT_KV_DIM, F32), (IDX_HEAD_DIM, F32),
              (ATT_KV_DIM, BF16), (ATT_KV_DIM, BF16), (IDX_HEAD_DIM, BF16)]
    return pl.pallas_call(
        _dsa_in_proj_kernel,
        grid=(m // tm,),
        in_specs=[pl.BlockSpec((tm, d), row),
                  pl.BlockSpec((1, d), lambda i: (0, 0)),
                  pl.BlockSpec((d, DSA_IN_PAD), lambda i: (0, 0), pipeline_mode=pl.Buffered(1))],
        out_specs=[pl.BlockSpec((tm, n), row) for n, _ in widths],
        out_shape=[jax.ShapeDtypeStruct((m, n), dt) for n, dt in widths],
        compiler_params=_cparams("parallel"),
        name="dsa_in_proj",
    )(h, g.reshape(1, d), w)


def _dsa_kernel(q_ref, qi_ref, kw_ref, k_ref, v_ref, ki_ref, o_ref, key_ref, bias_ref,
                *, tq, s, n_keys, q_start, topk):
    i = pl.program_id(1)
    kpos = lax.broadcasted_iota(jnp.int32, (tq, s), 1)
    qpos = q_start + i * tq + lax.broadcasted_iota(jnp.int32, (tq, s), 0)
    kchunk = jnp.where(kpos < n_keys, kpos >> 6, jnp.int32(2 ** 30))
    admissible = kchunk <= (qpos >> 6)

    if min(s, n_keys) <= topk:
        bias_ref[...] = jnp.where(admissible, 0.0, NEG)
    else:
        qi = qi_ref[...]
        wi = kw_ref[:, DSA_WI_LANE:DSA_WI_LANE + IDX_N_HEADS] * (IDX_N_HEADS ** -0.5)
        ki = ki_ref[0]
        score = jnp.zeros((tq, s), F32)
        for h in range(IDX_N_HEADS):
            d = _dot_nt(qi[:, h * IDX_HEAD_DIM:(h + 1) * IDX_HEAD_DIM], ki)
            score = score + wi[:, h:h + 1] * jnp.maximum(d, 0.0)
        score = jnp.where(admissible, score + 0.0, -jnp.inf)

        bits = pltpu.bitcast(score, jnp.int32)
        key_ref[...] = jnp.where(bits < 0, bits ^ jnp.int32(0x7FFFFFFF), bits)
        kf = jnp.float32(topk)

        def bisect(step, t):
            cand = t + jnp.left_shift(jnp.int32(1), 31 - step)
            cnt = jnp.sum(jnp.where(key_ref[...] >= cand, 1.0, 0.0), axis=1, keepdims=True)
            return jnp.where(cnt >= kf, cand, t)

        thr = lax.fori_loop(0, 32, bisect, jnp.full((tq, 1), INT_MIN, jnp.int32))

        key = key_ref[...]
        above = key > thr
        tie = key == thr
        need = kf - jnp.sum(jnp.where(above, 1.0, 0.0), axis=1, keepdims=True)
        upper = jnp.where(lax.broadcasted_iota(jnp.int32, (LANES, LANES), 0)
                          <= lax.broadcasted_iota(jnp.int32, (LANES, LANES), 1), 1.0, 0.0).astype(BF16)
        seen = jnp.zeros((tq, 1), F32)
        for blk in range(s // LANES):
            sl = slice(blk * LANES, (blk + 1) * LANES)
            tie_b = tie[:, sl]
            rank = _dot(jnp.where(tie_b, 1.0, 0.0).astype(BF16), upper) + seen
            keep = jnp.where(above[:, sl], 0.0, jnp.where(tie_b, jnp.where(rank <= need, 0.0, NEG), NEG))
            bias_ref[:, sl] = jnp.where(admissible[:, sl], keep, NEG)
            seen = rank[:, LANES - 1:LANES]

    qq = q_ref[...]
    bias = bias_ref[...]
    for g in range(ATT_N_KV_HEADS):
        heads = [qq[:, (g * ATT_REP + r) * ATT_HEAD_DIM:(g * ATT_REP + r + 1) * ATT_HEAD_DIM]
                 for r in range(ATT_REP)]
        qg = jnp.concatenate(heads, axis=0)
        kg = k_ref[0, :, g * ATT_HEAD_DIM:(g + 1) * ATT_HEAD_DIM]
        vg = v_ref[0, :, g * ATT_HEAD_DIM:(g + 1) * ATT_HEAD_DIM]
        logits = _dot_nt(qg, kg)
        logits = (logits.reshape(ATT_REP, tq, s) + bias[None]).reshape(ATT_REP * tq, s)
        m = jnp.max(logits, axis=1, keepdims=True)
        p = jnp.exp(logits - m)
        denom = jnp.sum(p, axis=1, keepdims=True)
        og = _dot(p.astype(BF16), vg) / denom
        for r in range(ATT_REP):
            hh = g * ATT_REP + r
            o_ref[0, :, hh * ATT_HEAD_DIM:(hh + 1) * ATT_HEAD_DIM] = og[r * tq:(r + 1) * tq].astype(BF16)


def dsa_core(q, qi, kw, k_all, v_all, ki_all, n_seq, length, n_keys, past, q_off, n_rows, tq):
    s_pad = k_all.shape[1]
    s = min(s_pad, -(-(past + q_off + n_rows) // LANES) * LANES)
    nq = n_rows // tq
    topk = min(TOPK_MAX, n_keys // 4)
    rows = lambda b, i: (b * (length // tq) + q_off // tq + i, 0)
    seq = lambda b, i: (b, 0, 0)
    return pl.pallas_call(
        functools.partial(_dsa_kernel, tq=tq, s=s, n_keys=n_keys, q_start=past + q_off, topk=topk),
        grid=(n_seq, nq),
        in_specs=[pl.BlockSpec((tq, ATT_Q_DIM), rows),
                  pl.BlockSpec((tq, IDX_Q_DIM), rows),
                  pl.BlockSpec((tq, LANES), rows),
                  pl.BlockSpec((1, s, ATT_KV_DIM), seq),
                  pl.BlockSpec((1, s, ATT_KV_DIM), seq),
                  pl.BlockSpec((1, s, IDX_HEAD_DIM), seq)],
        out_specs=pl.BlockSpec((1, tq, ATT_Q_DIM), lambda b, i: (b, i, 0)),
        out_shape=jax.ShapeDtypeStruct((n_seq, n_rows, ATT_Q_DIM), BF16),
        scratch_shapes=[pltpu.VMEM((tq, s), jnp.int32),
                        pltpu.VMEM((tq, s), F32)],
        compiler_params=_cparams("parallel", "arbitrary"),
        name="dsa_core",
    )(q, qi, kw, k_all, v_all, ki_all)


def dsa_mixer(h, n_seq, length, cache, g_mix, w):
    q, qi, kw, k_new, v_new, ki_new, kb, vb, kib = dsa_in_proj(h, g_mix, w["w_in"])
    parts = ([kb.reshape(n_seq, length, ATT_KV_DIM)], [vb.reshape(n_seq, length, ATT_KV_DIM)],
             [kib.reshape(n_seq, length, IDX_HEAD_DIM)])
    past = 0
    if cache is not None:
        cache_k, cache_v, cache_ki = cache
        past = cache_k.shape[1]
        olds = (cache_k.reshape(n_seq, past, ATT_KV_DIM), cache_v.reshape(n_seq, past, ATT_KV_DIM), cache_ki)
        parts = tuple([old.astype(BF16)] + new for old, new in zip(olds, parts))
    n_keys = past + length
    s_pad = -(-n_keys // LANES) * LANES

    def keys(ps):
        if s_pad > n_keys:
            ps = ps + [jnp.zeros((n_seq, s_pad - n_keys, ps[0].shape[2]), BF16)]
        return ps[0] if len(ps) == 1 else jnp.concatenate(ps, axis=1)

    k_all, v_all, ki_all = keys(parts[0]), keys(parts[1]), keys(parts[2])
    rows_per_call = min(length, 256)
    outs = [dsa_core(q, qi, kw, k_all, v_all, ki_all, n_seq, length, n_keys, past,
                     q_off, rows_per_call, rows_per_call)
            for q_off in range(0, length, rows_per_call)]
    o = outs[0] if len(outs) == 1 else jnp.concatenate(outs, axis=1)
    h = matmul_residual(o.reshape(n_seq * length, ATT_Q_DIM), w["w_out"], h)
    return (h, k_new.reshape(n_seq, length, ATT_N_KV_HEADS, ATT_HEAD_DIM),
            v_new.reshape(n_seq, length, ATT_N_KV_HEADS, ATT_HEAD_DIM),
            ki_new.reshape(n_seq, length, IDX_HEAD_DIM))


def _shortconv_kernel(gb_ref, gc_ref, u_ref, h_ref, carry0_ref, cw_ref, wout_ref,
                      o_ref, sout_ref, ext_ref, *, t, has_init):
    i = pl.program_id(1)
    halo = SUBLANES

    @pl.when(i == 0)
    def _():
        if has_init:
            ext_ref[0:halo, :] = carry0_ref[0]
        else:
            ext_ref[0:halo, :] = jnp.zeros((halo, D_MODEL), F32)

    ext_ref[halo:halo + t, :] = gc_ref[...] * u_ref[...]
    first = halo - (SC_WIDTH - 1)
    y = ext_ref[first:first + t, :] * cw_ref[0:1, :]
    for k in range(1, SC_WIDTH):
        y = y + ext_ref[first + k:first + k + t, :] * cw_ref[k:k + 1, :]
    ext_ref[0:halo, :] = ext_ref[t:t + halo, :]
    o_ref[...] = h_ref[...] + _dot((gb_ref[...] * y).astype(BF16), wout_ref[...])

    @pl.when(i == pl.num_programs(1) - 1)
    def _():
        sout_ref[0] = ext_ref[0:halo, :]


def shortconv_mixer(h, n_seq, length, conv_state, g_mix, w):
    proj = norm_matmul(h, g_mix, w["w_in"], tn=1536)
    t = min(length, 512)
    nt = length // t
    has_init = conv_state is not None
    if has_init:
        carry0 = jnp.concatenate(
            [jnp.zeros((n_seq, SUBLANES - (SC_WIDTH - 1), D_MODEL), F32), conv_state], axis=1)
        seq_map = lambda b, i: (b, 0, 0)
    else:
        carry0 = jnp.zeros((1, SUBLANES, D_MODEL), F32)
        seq_map = lambda b, i: (0, 0, 0)
    rows = lambda b, i: b * nt + i
    h_new, st = pl.pallas_call(
        functools.partial(_shortconv_kernel, t=t, has_init=has_init),
        grid=(n_seq, nt),
        in_specs=[pl.BlockSpec((t, D_MODEL), lambda b, i: (rows(b, i), 0)),
                  pl.BlockSpec((t, D_MODEL), lambda b, i: (rows(b, i), 1)),
                  pl.BlockSpec((t, D_MODEL), lambda b, i: (rows(b, i), 2)),
                  pl.BlockSpec((t, D_MODEL), lambda b, i: (rows(b, i), 0)),
                  pl.BlockSpec((1, SUBLANES, D_MODEL), seq_map),
                  pl.BlockSpec((SC_WIDTH, D_MODEL), lambda b, i: (0, 0)),
                  pl.BlockSpec((D_MODEL, D_MODEL), lambda b, i: (0, 0))],
        out_specs=[pl.BlockSpec((t, D_MODEL), lambda b, i: (rows(b, i), 0)),
                   pl.BlockSpec((1, SUBLANES, D_MODEL), lambda b, i: (b, 0, 0))],
        out_shape=[jax.ShapeDtypeStruct((n_seq * length, D_MODEL), F32),
                   jax.ShapeDtypeStruct((n_seq, SUBLANES, D_MODEL), F32)],
        scratch_shapes=[pltpu.VMEM((t + SUBLANES, D_MODEL), F32)],
        compiler_params=_cparams("parallel", "arbitrary"),
        name="shortconv_core",
    )(proj, proj, proj, h, carry0, w["conv_w"], w["w_out"])
    return h_new, st[:, SUBLANES - (SC_WIDTH - 1):, :]


def _trunk(x, states, weights, norm_mix_g, norm_ffn_g, norm_final_g, ffn_w):
    n_seq, length, _ = x.shape
    h = x.reshape(n_seq * length, D_MODEL)
    depth = len(weights)
    new_states = []
    for i in range(depth):
        kind = i % 3
        st = states[i]
        if kind == 0:
            conv_state, ssm_state = st if st is not None else (None, None)
            h, new_conv, new_ssm = ssd_mixer(h, n_seq, length, conv_state, ssm_state,
                                             norm_mix_g[i], weights[i])
            new_states.append((new_conv, new_ssm))
        elif kind == 1:
            h, k_new, v_new, ki_new = dsa_mixer(h, n_seq, length, st, norm_mix_g[i], weights[i])
            new_states.append((k_new, v_new, ki_new))
        else:
            h, new_conv = shortconv_mixer(h, n_seq, length, st[0] if st is not None else None,
                                          norm_mix_g[i], weights[i])
            new_states.append((new_conv,))
        wg, wu, wd = ffn_w[i]
        h = ffn(h, norm_ffn_g[i], wg, wu, wd, norm_final_g, final_norm=(i == depth - 1))
    return h.reshape(n_seq, length, D_MODEL), new_states


def kernel(x_prompt, x_sample, state_l0_conv, state_l0_ssm, cache_l1_k, cache_l1_v, cache_l1_kidx, state_l2_conv, state_l3_conv, state_l3_ssm, norm_mix_g, norm_ffn_g, norm_final_g, ffn_w_gate, ffn_w_up, ffn_w_down, l0_w_in, l0_conv_w, l0_conv_b, l0_dt_bias, l0_a_log, l0_d_skip, l0_norm_g, l0_w_out, l1_w_in, l1_w_out, l2_w_in, l2_conv_w, l2_w_out, l3_w_in, l3_conv_w, l3_conv_b, l3_dt_bias, l3_a_log, l3_d_skip, l3_norm_g, l3_w_out):
    depth = norm_mix_g.shape[0]
    weights = [
        prep_ssd_weights(l0_w_in, l0_conv_w, l0_conv_b, l0_dt_bias, l0_a_log, l0_d_skip, l0_norm_g, l0_w_out),
        dict(w_in=jnp.zeros((D_MODEL, DSA_IN_PAD), BF16).at[:, :DSA_IN_DIM].set(l1_w_in.astype(BF16)),
             w_out=l1_w_out.astype(BF16)),
        dict(w_in=l2_w_in.astype(BF16), conv_w=l2_conv_w, w_out=l2_w_out.astype(BF16)),
        prep_ssd_weights(l3_w_in, l3_conv_w, l3_conv_b, l3_dt_bias, l3_a_log, l3_d_skip, l3_norm_g, l3_w_out),
    ]
    ffn_w = [(ffn_w_gate[i].astype(BF16), ffn_w_up[i].astype(BF16), ffn_w_down[i].astype(BF16))
             for i in range(depth)]
    sample_states = [(state_l0_conv, state_l0_ssm), (cache_l1_k, cache_l1_v, cache_l1_kidx),
                     (state_l2_conv,), (state_l3_conv, state_l3_ssm)]
    prompt_states = [None] * depth
    y_p, new_p = _trunk(x_prompt, prompt_states, weights, norm_mix_g, norm_ffn_g, norm_final_g, ffn_w)
    y_s, new_s = _trunk(x_sample, sample_states, weights, norm_mix_g, norm_ffn_g, norm_final_g, ffn_w)
    (p_l0_conv, p_l0_ssm), (p_l1_k, p_l1_v, p_l1_kidx), (p_l2_conv,), (p_l3_conv, p_l3_ssm) = new_p
    (s_l0_conv, s_l0_ssm), (s_l1_k, s_l1_v, s_l1_kidx), (s_l2_conv,), (s_l3_conv, s_l3_ssm) = new_s
    return (y_p, y_s,
            p_l0_conv, p_l0_ssm, s_l0_conv, s_l0_ssm,
            p_l1_k, p_l1_v, p_l1_kidx, s_l1_k, s_l1_v, s_l1_kidx,
            p_l2_conv, s_l2_conv,
            p_l3_conv, p_l3_ssm, s_l3_conv, s_l3_ssm)
```

```python
import functools

import jax
import jax.numpy as jnp
from jax import lax
from jax.experimental import pallas as pl
from jax.experimental.pallas import tpu as pltpu

F32 = jnp.float32
BF16 = jnp.bfloat16

EPS = 1e-6
CHUNK = 64

D_MODEL = 1024
FFN_HIDDEN = 2816

SSD_D_INNER = 2048
SSD_HEAD_DIM = 64
SSD_N_HEADS = 32
SSD_N_GROUPS = 8
SSD_HEADS_PER_GROUP = 4
SSD_D_STATE = 128
SSD_CONV_W = 4
SSD_BC_DIM = SSD_N_GROUPS * SSD_D_STATE
SSD_CONV_DIM = SSD_D_INNER + 2 * SSD_BC_DIM
SSD_IN_DIM = SSD_D_INNER + SSD_CONV_DIM + SSD_N_HEADS
SSD_IN_PAD = 6400
SSD_GROUP_W = SSD_HEADS_PER_GROUP * SSD_HEAD_DIM
SSD_TILE = 256
CONV_BLOCK_ROWS = 64
CONV_BLOCK_COLS = 256

ATT_N_HEADS = 16
ATT_N_KV_HEADS = 4
ATT_REP = ATT_N_HEADS // ATT_N_KV_HEADS
ATT_HEAD_DIM = 64
ATT_Q_DIM = 1024
ATT_KV_DIM = 256
IDX_N_HEADS = 8
IDX_HEAD_DIM = 64
IDX_Q_DIM = 512
DSA_IN_DIM = 2120
DSA_IN_PAD = 2304
DSA_K_OFF = 1024
DSA_V_OFF = 1280
DSA_QI_OFF = 1536
DSA_KI_OFF = 2048
DSA_WI_LANE = 64
DSA_ROWS_PER_CALL = 256
TOPK_MAX = 256

SC_WIDTH = 3
SC_TILE = 512

LANES = 128
SUBLANES = 8
VMEM_LIMIT = 56 * 1024 * 1024
NEG = -1e30
INT_MIN = -(2 ** 31)


def _cparams(*sem):
    return pltpu.CompilerParams(dimension_semantics=sem, vmem_limit_bytes=VMEM_LIMIT)


def _resident(shape, index_map):
    return pl.BlockSpec(shape, index_map, pipeline_mode=pl.Buffered(1))


def _sigmoid(v):
    return 1.0 / (1.0 + jnp.exp(-v))


def _dot(a, b):
    return jnp.dot(a, b, preferred_element_type=F32)


def _dot_nt(a, b, precision=None):
    return lax.dot_general(a, b, (((1,), (1,)), ((), ())), precision=precision,
                           preferred_element_type=F32)


def _dot_tn(a, b):
    return lax.dot_general(a, b, (((0,), (0,)), ((), ())), preferred_element_type=F32)


def _rmsnorm_rows(x, g):
    ms = jnp.mean(x * x, axis=-1, keepdims=True)
    return x * lax.rsqrt(ms + EPS) * g


def _ffn_kernel(h_ref, g_ref, wg_ref, wu_ref, wd_ref, gf_ref, o_ref, *, final_norm):
    h = h_ref[...]
    xn = _rmsnorm_rows(h, g_ref[...]).astype(BF16)
    a = _dot(xn, wg_ref[...])
    b = _dot(xn, wu_ref[...])
    t = (a * _sigmoid(a) * b).astype(BF16)
    out = h + _dot(t, wd_ref[...])
    if final_norm:
        out = _rmsnorm_rows(out, gf_ref[...])
    o_ref[...] = out


def ffn(h, g, wg, wu, wd, gf, final_norm):
    m, d = h.shape
    f = wg.shape[1]
    tm = min(m, 512)
    return pl.pallas_call(
        functools.partial(_ffn_kernel, final_norm=final_norm),
        grid=(m // tm,),
        in_specs=[pl.BlockSpec((tm, d), lambda i: (i, 0)),
                  pl.BlockSpec((1, d), lambda i: (0, 0)),
                  _resident((d, f), lambda i: (0, 0)),
                  _resident((d, f), lambda i: (0, 0)),
                  _resident((f, d), lambda i: (0, 0)),
                  pl.BlockSpec((1, d), lambda i: (0, 0))],
        out_specs=pl.BlockSpec((tm, d), lambda i: (i, 0)),
        out_shape=jax.ShapeDtypeStruct((m, d), F32),
        compiler_params=_cparams("parallel"),
        name="ffn",
    )(h, g.reshape(1, d), wg, wu, wd, gf.reshape(1, d))


def _expand_heads(v):
    r = v.shape[0]
    low_half = lax.broadcasted_iota(jnp.int32, (r, LANES), 1) < SSD_HEAD_DIM
    cols = []
    for p in range(SSD_N_HEADS // 2):
        lo = jnp.broadcast_to(v[:, 2 * p:2 * p + 1], (r, LANES))
        hi = jnp.broadcast_to(v[:, 2 * p + 1:2 * p + 2], (r, LANES))
        cols.append(jnp.where(low_half, lo, hi))
    return jnp.concatenate(cols, axis=1)


def _ssd_chunk(r0, q, dt, da, xs_ref, xsb_ref, bm_ref, cm_ref, st_ref, yacc_ref):
    rows = slice(r0, r0 + q)
    row = lax.broadcasted_iota(jnp.int32, (q, q), 0)
    col = lax.broadcasted_iota(jnp.int32, (q, q), 1)
    causal = row >= col
    tril = jnp.where(causal, 1.0, 0.0).astype(F32)
    cum = jnp.dot(tril, da, precision=lax.Precision.HIGHEST, preferred_element_type=F32)
    eye = jnp.where(lax.broadcasted_iota(jnp.int32, (LANES, LANES), 0)
                    == lax.broadcasted_iota(jnp.int32, (LANES, LANES), 1), 1.0, 0.0).astype(F32)
    cum_t = _dot_nt(eye, cum, precision=lax.Precision.HIGHEST)
    dt_t = _dot_nt(eye, dt, precision=lax.Precision.HIGHEST)
    cum_last = cum[q - 1:q, :]
    w_end_x = _expand_heads(jnp.exp(cum_last - cum) * dt)
    e_cum_x = _expand_heads(jnp.exp(cum))
    dec_x = _expand_heads(jnp.exp(cum_last))
    xw_bf = (xs_ref[rows, :] * w_end_x).astype(BF16)

    for g in range(SSD_N_GROUPS):
        gl = slice(g * SSD_D_STATE, (g + 1) * SSD_D_STATE)
        gw = slice(g * SSD_GROUP_W, (g + 1) * SSD_GROUP_W)
        bg = bm_ref[rows, gl]
        cg = cm_ref[rows, gl]
        cb = _dot_nt(cg, bg)
        st_g = st_ref[g]
        y_inter = _dot(cg, st_g.astype(BF16)) * e_cum_x[:, gw]
        st_ref[g] = st_g * dec_x[:, gw] + _dot_tn(bg, xw_bf[:, gw])
        y_heads = []
        for r in range(SSD_HEADS_PER_GROUP):
            h = g * SSD_HEADS_PER_GROUP + r
            seg = cum[:, h:h + 1] - cum_t[h:h + 1, :]
            decay = jnp.exp(jnp.where(causal, seg, NEG))
            mix = (cb * decay * dt_t[h:h + 1, :]).astype(BF16)
            y_heads.append(_dot(mix, xsb_ref[rows, h * SSD_HEAD_DIM:(h + 1) * SSD_HEAD_DIM]))
        yacc_ref[rows, gw] = jnp.concatenate(y_heads, axis=1) + y_inter


def _ssd_layer_kernel(h_ref, g_ref, win_ref, carry0_ref, s0_ref,
                      convw_ref, convb_ref, dtb_ref, alog_ref, dskip_ref, normg_ref, wout_ref,
                      o_ref, cout_ref, sout_ref,
                      z_ref, ext_ref, st_ref, xs_ref, xsb_ref, bm_ref, cm_ref, yacc_ref,
                      *, t, q, nt, has_init):
    f = pl.program_id(0)
    i = lax.rem(f, nt)
    halo = SUBLANES
    taps = SSD_CONV_W

    @pl.when(i == 0)
    def _():
        if has_init:
            ext_ref[0:halo, :] = carry0_ref[0]
            for g in range(SSD_N_GROUPS):
                st_ref[g] = s0_ref[0, g].T
        else:
            ext_ref[0:halo, :] = jnp.zeros((halo, SSD_CONV_DIM), F32)
            st_ref[...] = jnp.zeros(st_ref.shape, F32)

    h = h_ref[...]
    proj = _dot(_rmsnorm_rows(h, g_ref[...]).astype(BF16), win_ref[...])
    z_ref[...] = proj[:, 0:SSD_D_INNER]
    ext_ref[halo:halo + t, :] = proj[:, SSD_D_INNER:SSD_D_INNER + SSD_CONV_DIM]

    first = halo - (taps - 1)
    for c0 in range(0, SSD_CONV_DIM, CONV_BLOCK_COLS):
        cols = slice(c0, c0 + CONV_BLOCK_COLS)
        wk = [convw_ref[k:k + 1, cols] for k in range(taps)]
        bias = convb_ref[:, cols]
        for r0 in range(0, t, CONV_BLOCK_ROWS):
            acc = bias + ext_ref[first + r0:first + r0 + CONV_BLOCK_ROWS, cols] * wk[0]
            for k in range(1, taps):
                acc = acc + ext_ref[first + k + r0:first + k + r0 + CONV_BLOCK_ROWS, cols] * wk[k]
            val = acc * _sigmoid(acc)
            rows = slice(r0, r0 + CONV_BLOCK_ROWS)
            if c0 < SSD_D_INNER:
                xs_ref[rows, cols] = val
                xsb_ref[rows, cols] = val.astype(BF16)
            elif c0 < SSD_D_INNER + SSD_BC_DIM:
                bm_ref[rows, c0 - SSD_D_INNER:c0 - SSD_D_INNER + CONV_BLOCK_COLS] = val.astype(BF16)
            else:
                off = c0 - SSD_D_INNER - SSD_BC_DIM
                cm_ref[rows, off:off + CONV_BLOCK_COLS] = val.astype(BF16)
    ext_ref[0:halo, :] = ext_ref[t:t + halo, :]
    v = proj[:, SSD_D_INNER + SSD_CONV_DIM:SSD_D_INNER + SSD_CONV_DIM + LANES] + dtb_ref[...]
    dt = jnp.maximum(v, 0.0) + jnp.log1p(jnp.exp(-jnp.abs(v)))
    head_lane = lax.broadcasted_iota(jnp.int32, (1, LANES), 1) < SSD_N_HEADS
    da = dt * jnp.where(head_lane, -jnp.exp(alog_ref[...]), 0.0)

    for r0 in range(0, t, q):
        _ssd_chunk(r0, q, dt[r0:r0 + q], da[r0:r0 + q], xs_ref, xsb_ref, bm_ref, cm_ref, st_ref, yacc_ref)

    zz = z_ref[...]
    y = (yacc_ref[...] + dskip_ref[...] * xs_ref[...]) * (zz * _sigmoid(zz))
    normed = []
    for g in range(SSD_N_GROUPS):
        gw = slice(g * SSD_GROUP_W, (g + 1) * SSD_GROUP_W)
        yg = y[:, gw]
        ms = jnp.mean(yg * yg, axis=-1, keepdims=True)
        normed.append((yg * lax.rsqrt(ms + EPS) * normg_ref[:, gw]).astype(BF16))
    o_ref[...] = h + _dot(jnp.concatenate(normed, axis=1), wout_ref[...])

    @pl.when(i == nt - 1)
    def _():
        cout_ref[0] = ext_ref[0:halo, :]
        for g in range(SSD_N_GROUPS):
            sout_ref[0, g] = st_ref[g].T


def ssd_layer(h, n_seq, length, conv_state, ssm_state, g_mix, w):
    t = min(length, SSD_TILE)
    has_init = conv_state is not None
    state_io_shape = (SSD_N_GROUPS, SSD_GROUP_W, SSD_D_STATE)
    if has_init:
        carry0 = jnp.concatenate(
            [jnp.zeros((n_seq, SUBLANES - (SSD_CONV_W - 1), SSD_CONV_DIM), F32), conv_state], axis=1)
        s0 = ssm_state.reshape((n_seq,) + state_io_shape)
        seq_in3 = lambda f: (f // nt, 0, 0)
        seq_in4 = lambda f: (f // nt, 0, 0, 0)
    else:
        carry0 = jnp.zeros((1, SUBLANES, SSD_CONV_DIM), F32)
        s0 = jnp.zeros((1,) + state_io_shape, F32)
        seq_in3 = lambda f: (0, 0, 0)
        seq_in4 = lambda f: (0, 0, 0, 0)
    nt = length // t
    n_tiles = n_seq * nt
    rows = lambda f: (f, 0)
    const2 = lambda f: (0, 0)
    h_new, conv_out, st = pl.pallas_call(
        functools.partial(_ssd_layer_kernel, t=t, q=CHUNK, nt=nt, has_init=has_init),
        grid=(n_tiles,),
        in_specs=[pl.BlockSpec((t, D_MODEL), rows),
                  pl.BlockSpec((1, D_MODEL), const2),
                  _resident((D_MODEL, SSD_IN_PAD), const2),
                  pl.BlockSpec((1, SUBLANES, SSD_CONV_DIM), seq_in3),
                  pl.BlockSpec((1,) + state_io_shape, seq_in4),
                  pl.BlockSpec((SSD_CONV_W, SSD_CONV_DIM), const2),
                  pl.BlockSpec((1, SSD_CONV_DIM), const2),
                  pl.BlockSpec((1, LANES), const2),
                  pl.BlockSpec((1, LANES), const2),
                  pl.BlockSpec((1, SSD_D_INNER), const2),
                  pl.BlockSpec((1, SSD_D_INNER), const2),
                  _resident((SSD_D_INNER, D_MODEL), const2)],
        out_specs=[pl.BlockSpec((t, D_MODEL), rows),
                   pl.BlockSpec((1, SUBLANES, SSD_CONV_DIM), lambda f: (f // nt, 0, 0)),
                   pl.BlockSpec((1,) + state_io_shape, lambda f: (f // nt, 0, 0, 0))],
        out_shape=[jax.ShapeDtypeStruct((n_seq * length, D_MODEL), F32),
                   jax.ShapeDtypeStruct((n_seq, SUBLANES, SSD_CONV_DIM), F32),
                   jax.ShapeDtypeStruct((n_seq,) + state_io_shape, F32)],
        scratch_shapes=[pltpu.VMEM((t, SSD_D_INNER), F32),
                        pltpu.VMEM((t + SUBLANES, SSD_CONV_DIM), F32),
                        pltpu.VMEM((SSD_N_GROUPS, SSD_D_STATE, SSD_GROUP_W), F32),
                        pltpu.VMEM((t, SSD_D_INNER), F32),
                        pltpu.VMEM((t, SSD_D_INNER), BF16),
                        pltpu.VMEM((t, SSD_BC_DIM), BF16),
                        pltpu.VMEM((t, SSD_BC_DIM), BF16),
                        pltpu.VMEM((t, SSD_D_INNER), F32)],
        compiler_params=_cparams("arbitrary"),
        name="ssd_layer",
    )(h, g_mix.reshape(1, D_MODEL), w["w_in"], carry0, s0, w["conv_w"], w["conv_b"], w["dt_bias"],
      w["a_log"], w["d_skip_x"], w["norm_g"], w["w_out"])
    return (h_new, conv_out[:, SUBLANES - (SSD_CONV_W - 1):, :],
            st.reshape(n_seq, SSD_N_HEADS, SSD_HEAD_DIM, SSD_D_STATE))


def _pad_lanes(v):
    return jnp.zeros((1, LANES), F32).at[0, :v.shape[0]].set(v)


def prep_ssd_weights(w_in, conv_w, conv_b, dt_bias, a_log, d_skip, norm_g, w_out):
    w_in_p = jnp.zeros((D_MODEL, SSD_IN_PAD), BF16).at[:, :SSD_IN_DIM].set(w_in.astype(BF16))
    return dict(w_in=w_in_p, conv_w=conv_w, conv_b=conv_b.reshape(1, SSD_CONV_DIM),
                dt_bias=_pad_lanes(dt_bias), a_log=_pad_lanes(a_log),
                d_skip_x=jnp.repeat(d_skip, SSD_HEAD_DIM).reshape(1, SSD_D_INNER),
                norm_g=norm_g.reshape(1, SSD_D_INNER), w_out=w_out.astype(BF16))


def _dsa_in_proj_kernel(x_ref, g_ref, w_ref, q_ref, qi_ref, kw_ref, k_ref, v_ref, ki_ref,
                        kb_ref, vb_ref, kib_ref):
    proj = _dot(_rmsnorm_rows(x_ref[...], g_ref[...]).astype(BF16), w_ref[...])
    q_ref[...] = (proj[:, 0:ATT_Q_DIM] * (ATT_HEAD_DIM ** -0.5)).astype(BF16)
    qi_ref[...] = (proj[:, DSA_QI_OFF:DSA_QI_OFF + IDX_Q_DIM] * (IDX_HEAD_DIM ** -0.5)).astype(BF16)
    kw_ref[...] = proj[:, DSA_KI_OFF:DSA_KI_OFF + LANES]
    k = proj[:, DSA_K_OFF:DSA_K_OFF + ATT_KV_DIM]
    v = proj[:, DSA_V_OFF:DSA_V_OFF + ATT_KV_DIM]
    ki = proj[:, DSA_KI_OFF:DSA_KI_OFF + IDX_HEAD_DIM]
    k_ref[...] = k
    v_ref[...] = v
    ki_ref[...] = ki
    kb_ref[...] = k.astype(BF16)
    vb_ref[...] = v.astype(BF16)
    kib_ref[...] = ki.astype(BF16)


def dsa_in_proj(h, g, w):
    m, d = h.shape
    tm = min(m, 512)
    row = lambda i: (i, 0)
    widths = [(ATT_Q_DIM, BF16), (IDX_Q_DIM, BF16), (LANES, F32),
              (ATT_KV_DIM, F32), (ATT_KV_DIM, F32), (IDX_HEAD_DIM, F32),
              (ATT_KV_DIM, BF16), (ATT_KV_DIM, BF16), (IDX_HEAD_DIM, BF16)]
    return pl.pallas_call(
        _dsa_in_proj_kernel,
        grid=(m // tm,),
        in_specs=[pl.BlockSpec((tm, d), row),
                  pl.BlockSpec((1, d), lambda i: (0, 0)),
                  _resident((d, DSA_IN_PAD), lambda i: (0, 0))],
        out_specs=[pl.BlockSpec((tm, n), row) for n, _ in widths],
        out_shape=[jax.ShapeDtypeStruct((m, n), dt) for n, dt in widths],
        compiler_params=_cparams("parallel"),
        name="dsa_in_proj",
    )(h, g.reshape(1, d), w)


def _dsa_kernel(q_ref, qi_ref, kw_ref, k_ref, v_ref, ki_ref, h_ref, wout_ref, o_ref,
                key_ref, bias_ref, att_ref, *, tq, s, n_keys, q_start, topk):
    i = pl.program_id(1)
    kpos = lax.broadcasted_iota(jnp.int32, (tq, s), 1)
    qpos = q_start + i * tq + lax.broadcasted_iota(jnp.int32, (tq, s), 0)
    kchunk = jnp.where(kpos < n_keys, kpos >> 6, jnp.int32(2 ** 30))
    admissible = kchunk <= (qpos >> 6)

    if min(s, n_keys) <= topk:
        bias_ref[...] = jnp.where(admissible, 0.0, NEG)
    else:
        qi = qi_ref[...]
        wi = kw_ref[:, DSA_WI_LANE:DSA_WI_LANE + IDX_N_HEADS] * (IDX_N_HEADS ** -0.5)
        ki = ki_ref[0]
        score = jnp.zeros((tq, s), F32)
        for h in range(IDX_N_HEADS):
            d = _dot_nt(qi[:, h * IDX_HEAD_DIM:(h + 1) * IDX_HEAD_DIM], ki)
            score = score + wi[:, h:h + 1] * jnp.maximum(d, 0.0)
        score = jnp.where(admissible, score + 0.0, -jnp.inf)

        bits = pltpu.bitcast(score, jnp.int32)
        key_ref[...] = jnp.where(bits < 0, bits ^ jnp.int32(0x7FFFFFFF), bits)
        kf = jnp.float32(topk)

        def bisect(step, t):
            cand = t + jnp.left_shift(jnp.int32(1), 31 - step)
            cnt = jnp.sum(jnp.where(key_ref[...] >= cand, 1.0, 0.0), axis=1, keepdims=True)
            return jnp.where(cnt >= kf, cand, t)

        thr = lax.fori_loop(0, 32, bisect, jnp.full((tq, 1), INT_MIN, jnp.int32))

        key = key_ref[...]
        above = key > thr
        tie = key == thr
        need = kf - jnp.sum(jnp.where(above, 1.0, 0.0), axis=1, keepdims=True)
        upper = jnp.where(lax.broadcasted_iota(jnp.int32, (LANES, LANES), 0)
                          <= lax.broadcasted_iota(jnp.int32, (LANES, LANES), 1), 1.0, 0.0).astype(BF16)
        seen = jnp.zeros((tq, 1), F32)
        for blk in range(s // LANES):
            sl = slice(blk * LANES, (blk + 1) * LANES)
            tie_b = tie[:, sl]
            rank = _dot(jnp.where(tie_b, 1.0, 0.0).astype(BF16), upper) + seen
            keep = jnp.where(above[:, sl], 0.0, jnp.where(tie_b, jnp.where(rank <= need, 0.0, NEG), NEG))
            bias_ref[:, sl] = jnp.where(admissible[:, sl], keep, NEG)
            seen = rank[:, LANES - 1:LANES]

    qq = q_ref[...]
    bias = bias_ref[...]
    for g in range(ATT_N_KV_HEADS):
        heads = [qq[:, (g * ATT_REP + r) * ATT_HEAD_DIM:(g * ATT_REP + r + 1) * ATT_HEAD_DIM]
                 for r in range(ATT_REP)]
        qg = jnp.concatenate(heads, axis=0)
        kg = k_ref[0, :, g * ATT_HEAD_DIM:(g + 1) * ATT_HEAD_DIM]
        vg = v_ref[0, :, g * ATT_HEAD_DIM:(g + 1) * ATT_HEAD_DIM]
        logits = _dot_nt(qg, kg)
        logits = (logits.reshape(ATT_REP, tq, s) + bias[None]).reshape(ATT_REP * tq, s)
        m = jnp.max(logits, axis=1, keepdims=True)
        p = jnp.exp(logits - m)
        denom = jnp.sum(p, axis=1, keepdims=True)
        og = _dot(p.astype(BF16), vg) / denom
        for r in range(ATT_REP):
            hh = g * ATT_REP + r
            att_ref[:, hh * ATT_HEAD_DIM:(hh + 1) * ATT_HEAD_DIM] = og[r * tq:(r + 1) * tq].astype(BF16)

    o_ref[...] = h_ref[...] + _dot(att_ref[...], wout_ref[...])


def dsa_core(q, qi, kw, k_all, v_all, ki_all, h, w_out, n_seq, length, n_keys, past, q_off, n_rows, tq):
    s_pad = k_all.shape[1]
    s = min(s_pad, -(-(past + q_off + n_rows) // LANES) * LANES)
    nq = n_rows // tq
    topk = min(TOPK_MAX, n_keys // 4)
    rows = lambda b, i: (b * (length // tq) + q_off // tq + i, 0)
    seq = lambda b, i: (b, 0, 0)
    return pl.pallas_call(
        functools.partial(_dsa_kernel, tq=tq, s=s, n_keys=n_keys, q_start=past + q_off, topk=topk),
        grid=(n_seq, nq),
        in_specs=[pl.BlockSpec((tq, ATT_Q_DIM), rows),
                  pl.BlockSpec((tq, IDX_Q_DIM), rows),
                  pl.BlockSpec((tq, LANES), rows),
                  pl.BlockSpec((1, s, ATT_KV_DIM), seq),
                  pl.BlockSpec((1, s, ATT_KV_DIM), seq),
                  pl.BlockSpec((1, s, IDX_HEAD_DIM), seq),
                  pl.BlockSpec((tq, D_MODEL), rows),
                  _resident((ATT_Q_DIM, D_MODEL), lambda b, i: (0, 0))],
        out_specs=pl.BlockSpec((tq, D_MODEL), rows),
        out_shape=jax.ShapeDtypeStruct((n_seq * length, D_MODEL), F32),
        input_output_aliases={6: 0},
        scratch_shapes=[pltpu.VMEM((tq, s), jnp.int32),
                        pltpu.VMEM((tq, s), F32),
                        pltpu.VMEM((tq, ATT_Q_DIM), BF16)],
        compiler_params=_cparams("parallel", "arbitrary"),
        name="dsa_core",
    )(q, qi, kw, k_all, v_all, ki_all, h, w_out)


def dsa_layer(h, n_seq, length, cache, g_mix, w):
    q, qi, kw, k_new, v_new, ki_new, kb, vb, kib = dsa_in_proj(h, g_mix, w["w_in"])
    parts = ([kb.reshape(n_seq, length, ATT_KV_DIM)], [vb.reshape(n_seq, length, ATT_KV_DIM)],
             [kib.reshape(n_seq, length, IDX_HEAD_DIM)])
    past = 0
    if cache is not None:
        cache_k, cache_v, cache_ki = cache
        past = cache_k.shape[1]
        olds = (cache_k.reshape(n_seq, past, ATT_KV_DIM), cache_v.reshape(n_seq, past, ATT_KV_DIM), cache_ki)
        parts = tuple([old.astype(BF16)] + new for old, new in zip(olds, parts))
    n_keys = past + length
    s_pad = -(-n_keys // LANES) * LANES

    def keys(ps):
        if s_pad > n_keys:
            ps = ps + [jnp.zeros((n_seq, s_pad - n_keys, ps[0].shape[2]), BF16)]
        return ps[0] if len(ps) == 1 else jnp.concatenate(ps, axis=1)

    k_all, v_all, ki_all = keys(parts[0]), keys(parts[1]), keys(parts[2])
    rows_per_call = min(length, DSA_ROWS_PER_CALL)
    for q_off in range(0, length, rows_per_call):
        h = dsa_core(q, qi, kw, k_all, v_all, ki_all, h, w["w_out"], n_seq, length, n_keys, past,
                     q_off, rows_per_call, rows_per_call)
    return (h, k_new.reshape(n_seq, length, ATT_N_KV_HEADS, ATT_HEAD_DIM),
            v_new.reshape(n_seq, length, ATT_N_KV_HEADS, ATT_HEAD_DIM),
            ki_new.reshape(n_seq, length, IDX_HEAD_DIM))


def _shortconv_kernel(h_ref, g_ref, win_ref, carry0_ref, cw_ref, wout_ref,
                      o_ref, sout_ref, ext_ref, *, t, has_init):
    i = pl.program_id(1)
    halo = SUBLANES

    @pl.when(i == 0)
    def _():
        if has_init:
            ext_ref[0:halo, :] = carry0_ref[0]
        else:
            ext_ref[0:halo, :] = jnp.zeros((halo, D_MODEL), F32)

    h = h_ref[...]
    proj = _dot(_rmsnorm_rows(h, g_ref[...]).astype(BF16), win_ref[...])
    ext_ref[halo:halo + t, :] = proj[:, D_MODEL:2 * D_MODEL] * proj[:, 2 * D_MODEL:3 * D_MODEL]
    first = halo - (SC_WIDTH - 1)
    y = ext_ref[first:first + t, :] * cw_ref[0:1, :]
    for k in range(1, SC_WIDTH):
        y = y + ext_ref[first + k:first + k + t, :] * cw_ref[k:k + 1, :]
    ext_ref[0:halo, :] = ext_ref[t:t + halo, :]
    o_ref[...] = h + _dot((proj[:, 0:D_MODEL] * y).astype(BF16), wout_ref[...])

    @pl.when(i == pl.num_programs(1) - 1)
    def _():
        sout_ref[0] = ext_ref[0:halo, :]


def shortconv_layer(h, n_seq, length, conv_state, g_mix, w):
    t = min(length, SC_TILE)
    nt = length // t
    has_init = conv_state is not None
    if has_init:
        carry0 = jnp.concatenate(
            [jnp.zeros((n_seq, SUBLANES - (SC_WIDTH - 1), D_MODEL), F32), conv_state], axis=1)
        seq_map = lambda b, i: (b, 0, 0)
    else:
        carry0 = jnp.zeros((1, SUBLANES, D_MODEL), F32)
        seq_map = lambda b, i: (0, 0, 0)
    rows = lambda b, i: (b * nt + i, 0)
    const2 = lambda b, i: (0, 0)
    h_new, st = pl.pallas_call(
        functools.partial(_shortconv_kernel, t=t, has_init=has_init),
        grid=(n_seq, nt),
        in_specs=[pl.BlockSpec((t, D_MODEL), rows),
                  pl.BlockSpec((1, D_MODEL), const2),
                  _resident((D_MODEL, 3 * D_MODEL), const2),
                  pl.BlockSpec((1, SUBLANES, D_MODEL), seq_map),
                  pl.BlockSpec((SC_WIDTH, D_MODEL), const2),
                  _resident((D_MODEL, D_MODEL), const2)],
        out_specs=[pl.BlockSpec((t, D_MODEL), rows),
                   pl.BlockSpec((1, SUBLANES, D_MODEL), lambda b, i: (b, 0, 0))],
        out_shape=[jax.ShapeDtypeStruct((n_seq * length, D_MODEL), F32),
                   jax.ShapeDtypeStruct((n_seq, SUBLANES, D_MODEL), F32)],
        scratch_shapes=[pltpu.VMEM((t + SUBLANES, D_MODEL), F32)],
        compiler_params=_cparams("parallel", "arbitrary"),
        name="shortconv_layer",
    )(h, g_mix.reshape(1, D_MODEL), w["w_in"], carry0, w["conv_w"], w["w_out"])
    return h_new, st[:, SUBLANES - (SC_WIDTH - 1):, :]


def _trunk(x, states, weights, norm_mix_g, norm_ffn_g, norm_final_g, ffn_w):
    n_seq, length, _ = x.shape
    h = x.reshape(n_seq * length, D_MODEL)
    depth = len(weights)
    new_states = []
    for i in range(depth):
        kind = i % 3
        st = states[i]
        if kind == 0:
            conv_state, ssm_state = st if st is not None else (None, None)
            h, new_conv, new_ssm = ssd_layer(h, n_seq, length, conv_state, ssm_state,
                                             norm_mix_g[i], weights[i])
            new_states.append((new_conv, new_ssm))
        elif kind == 1:
            h, k_new, v_new, ki_new = dsa_layer(h, n_seq, length, st, norm_mix_g[i], weights[i])
            new_states.append((k_new, v_new, ki_new))
        else:
            h, new_conv = shortconv_layer(h, n_seq, length, st[0] if st is not None else None,
                                          norm_mix_g[i], weights[i])
            new_states.append((new_conv,))
        wg, wu, wd = ffn_w[i]
        h = ffn(h, norm_ffn_g[i], wg, wu, wd, norm_final_g, final_norm=(i == depth - 1))
    return h.reshape(n_seq, length, D_MODEL), new_states


def kernel(x_prompt, x_sample, state_l0_conv, state_l0_ssm, cache_l1_k, cache_l1_v, cache_l1_kidx, state_l2_conv, state_l3_conv, state_l3_ssm, norm_mix_g, norm_ffn_g, norm_final_g, ffn_w_gate, ffn_w_up, ffn_w_down, l0_w_in, l0_conv_w, l0_conv_b, l0_dt_bias, l0_a_log, l0_d_skip, l0_norm_g, l0_w_out, l1_w_in, l1_w_out, l2_w_in, l2_conv_w, l2_w_out, l3_w_in, l3_conv_w, l3_conv_b, l3_dt_bias, l3_a_log, l3_d_skip, l3_norm_g, l3_w_out):
    depth = norm_mix_g.shape[0]
    weights = [
        prep_ssd_weights(l0_w_in, l0_conv_w, l0_conv_b, l0_dt_bias, l0_a_log, l0_d_skip, l0_norm_g, l0_w_out),
        dict(w_in=jnp.zeros((D_MODEL, DSA_IN_PAD), BF16).at[:, :DSA_IN_DIM].set(l1_w_in.astype(BF16)),
             w_out=l1_w_out.astype(BF16)),
        dict(w_in=l2_w_in.astype(BF16), conv_w=l2_conv_w, w_out=l2_w_out.astype(BF16)),
        prep_ssd_weights(l3_w_in, l3_conv_w, l3_conv_b, l3_dt_bias, l3_a_log, l3_d_skip, l3_norm_g, l3_w_out),
    ]
    ffn_w = [(ffn_w_gate[i].astype(BF16), ffn_w_up[i].astype(BF16), ffn_w_down[i].astype(BF16))
             for i in range(depth)]
    sample_states = [(state_l0_conv, state_l0_ssm), (cache_l1_k, cache_l1_v, cache_l1_kidx),
                     (state_l2_conv,), (state_l3_conv, state_l3_ssm)]
    prompt_states = [None] * depth
    y_p, new_p = _trunk(x_prompt, prompt_states, weights, norm_mix_g, norm_ffn_g, norm_final_g, ffn_w)
    y_s, new_s = _trunk(x_sample, sample_states, weights, norm_mix_g, norm_ffn_g, norm_final_g, ffn_w)
    (p_l0_conv, p_l0_ssm), (p_l1_k, p_l1_v, p_l1_kidx), (p_l2_conv,), (p_l3_conv, p_l3_ssm) = new_p
    (s_l0_conv, s_l0_ssm), (s_l1_k, s_l1_v, s_l1_kidx), (s_l2_conv,), (s_l3_conv, s_l3_ssm) = new_s
    return (y_p, y_s,
            p_l0_conv, p_l0_ssm, s_l0_conv, s_l0_ssm,
            p_l1_k, p_l1_v, p_l1_kidx, s_l1_k, s_l1_v, s_l1_kidx,
            p_l2_conv, s_l2_conv,
            p_l3_conv, p_l3_ssm, s_l3_conv, s_l3_ssm)
```

```python
import functools

import jax
import jax.numpy as jnp
from jax import lax
from jax.experimental import pallas as pl
from jax.experimental.pallas import tpu as pltpu

F32 = jnp.float32
BF16 = jnp.bfloat16

EPS = 1e-6
CHUNK = 64

D_MODEL = 1024
FFN_HIDDEN = 2816

SSD_D_INNER = 2048
SSD_HEAD_DIM = 64
SSD_N_HEADS = 32
SSD_N_GROUPS = 8
SSD_HEADS_PER_GROUP = 4
SSD_D_STATE = 128
SSD_CONV_W = 4
SSD_BC_DIM = SSD_N_GROUPS * SSD_D_STATE
SSD_CONV_DIM = SSD_D_INNER + 2 * SSD_BC_DIM
SSD_IN_DIM = SSD_D_INNER + SSD_CONV_DIM + SSD_N_HEADS
SSD_IN_PAD = 6400
SSD_GROUP_W = SSD_HEADS_PER_GROUP * SSD_HEAD_DIM
SSD_TILE = 256
CONV_BLOCK_ROWS = 64
CONV_BLOCK_COLS = 256

ATT_N_HEADS = 16
ATT_N_KV_HEADS = 4
ATT_REP = ATT_N_HEADS // ATT_N_KV_HEADS
ATT_HEAD_DIM = 64
ATT_Q_DIM = 1024
ATT_KV_DIM = 256
IDX_N_HEADS = 8
IDX_HEAD_DIM = 64
IDX_Q_DIM = 512
DSA_IN_DIM = 2120
DSA_IN_PAD = 2304
DSA_K_OFF = 1024
DSA_V_OFF = 1280
DSA_QI_OFF = 1536
DSA_KI_OFF = 2048
DSA_WI_LANE = 64
DSA_ROWS_PER_CALL = 256
TOPK_MAX = 256

SC_WIDTH = 3
SC_TILE = 512

LANES = 128
SUBLANES = 8
VMEM_LIMIT = 56 * 1024 * 1024
NEG = -1e30
INT_MIN = -(2 ** 31)


def _cparams(*sem):
    return pltpu.CompilerParams(dimension_semantics=sem, vmem_limit_bytes=VMEM_LIMIT)


def _resident(shape, index_map):
    return pl.BlockSpec(shape, index_map, pipeline_mode=pl.Buffered(1))


def _sigmoid(v):
    return 1.0 / (1.0 + jnp.exp(-v))


def _dot(a, b):
    return jnp.dot(a, b, preferred_element_type=F32)


def _dot_nt(a, b, precision=None):
    return lax.dot_general(a, b, (((1,), (1,)), ((), ())), precision=precision,
                           preferred_element_type=F32)


def _dot_tn(a, b):
    return lax.dot_general(a, b, (((0,), (0,)), ((), ())), preferred_element_type=F32)


def _rmsnorm_rows(x, g):
    ms = jnp.mean(x * x, axis=-1, keepdims=True)
    return x * lax.rsqrt(ms + EPS) * g


def _ffn_kernel(h_ref, g_ref, wg_ref, wu_ref, wd_ref, gf_ref, o_ref, *, final_norm):
    h = h_ref[...]
    xn = _rmsnorm_rows(h, g_ref[...]).astype(BF16)
    a = _dot(xn, wg_ref[...])
    b = _dot(xn, wu_ref[...])
    t = (a * _sigmoid(a) * b).astype(BF16)
    out = h + _dot(t, wd_ref[...])
    if final_norm:
        out = _rmsnorm_rows(out, gf_ref[...])
    o_ref[...] = out


def ffn(h, g, wg, wu, wd, gf, final_norm):
    m, d = h.shape
    f = wg.shape[1]
    tm = min(m, 512)
    return pl.pallas_call(
        functools.partial(_ffn_kernel, final_norm=final_norm),
        grid=(m // tm,),
        in_specs=[pl.BlockSpec((tm, d), lambda i: (i, 0)),
                  pl.BlockSpec((1, d), lambda i: (0, 0)),
                  _resident((d, f), lambda i: (0, 0)),
                  _resident((d, f), lambda i: (0, 0)),
                  _resident((f, d), lambda i: (0, 0)),
                  pl.BlockSpec((1, d), lambda i: (0, 0))],
        out_specs=pl.BlockSpec((tm, d), lambda i: (i, 0)),
        out_shape=jax.ShapeDtypeStruct((m, d), F32),
        compiler_params=_cparams("parallel"),
        name="ffn",
    )(h, g.reshape(1, d), wg, wu, wd, gf.reshape(1, d))


def _expand_heads(v):
    r = v.shape[0]
    low_half = lax.broadcasted_iota(jnp.int32, (r, LANES), 1) < SSD_HEAD_DIM
    cols = []
    for p in range(SSD_N_HEADS // 2):
        lo = jnp.broadcast_to(v[:, 2 * p:2 * p + 1], (r, LANES))
        hi = jnp.broadcast_to(v[:, 2 * p + 1:2 * p + 2], (r, LANES))
        cols.append(jnp.where(low_half, lo, hi))
    return jnp.concatenate(cols, axis=1)


def _ssd_consts(q):
    row = lax.broadcasted_iota(jnp.int32, (q, q), 0)
    col = lax.broadcasted_iota(jnp.int32, (q, q), 1)
    tril = jnp.where(row >= col, 1.0, 0.0).astype(F32)
    eye = jnp.where(lax.broadcasted_iota(jnp.int32, (LANES, LANES), 0)
                    == lax.broadcasted_iota(jnp.int32, (LANES, LANES), 1), 1.0, 0.0).astype(F32)
    i_w = lax.broadcasted_iota(jnp.int32, (q, SSD_GROUP_W), 0)
    j_w = lax.broadcasted_iota(jnp.int32, (q, SSD_GROUP_W), 1) & (q - 1)
    causal_w = i_w >= j_w
    r_row = lax.broadcasted_iota(jnp.int32, (SSD_GROUP_W, SSD_GROUP_W), 0) // q
    r_col = lax.broadcasted_iota(jnp.int32, (SSD_GROUP_W, SSD_GROUP_W), 1) // SSD_HEAD_DIM
    head_block = jnp.where(r_row == r_col, 1.0, 0.0).astype(BF16)
    return tril, eye, causal_w, head_block


def _ssd_chunk(r0, q, dt, da, consts, xs_ref, xsb_ref, bm_ref, cm_ref, st_ref, yacc_ref):
    tril, eye, causal_w, head_block = consts
    rows = slice(r0, r0 + q)
    nrep = SSD_HEADS_PER_GROUP
    cum = jnp.dot(tril, da, precision=lax.Precision.HIGHEST, preferred_element_type=F32)
    cum_t = _dot_nt(eye, cum, precision=lax.Precision.HIGHEST)
    dt_t = _dot_nt(eye, dt, precision=lax.Precision.HIGHEST)
    cum_last = cum[q - 1:q, :]
    cum_x = _expand_heads(cum)
    e_cum_x = jnp.exp(cum_x)
    w_end_x = _expand_heads(jnp.exp(cum_last - cum) * dt)
    dec_x = _expand_heads(jnp.exp(cum_last))
    xw_bf = (xs_ref[rows, :] * w_end_x).astype(BF16)

    for g in range(SSD_N_GROUPS):
        gl = slice(g * SSD_D_STATE, (g + 1) * SSD_D_STATE)
        gw = slice(g * SSD_GROUP_W, (g + 1) * SSD_GROUP_W)
        heads = range(g * nrep, (g + 1) * nrep)
        bg = bm_ref[rows, gl]
        cg = cm_ref[rows, gl]
        cum_src = jnp.concatenate([cum_t[h:h + 1, :] for h in heads], axis=1)
        dt_src = jnp.concatenate([dt_t[h:h + 1, :] for h in heads], axis=1)
        cb = _dot_nt(cg, jnp.concatenate([bg] * nrep, axis=0))
        decay = jnp.exp(jnp.where(causal_w, cum_x[:, gw] - cum_src, NEG))
        mix = (cb * decay * dt_src).astype(BF16)
        x_blocks = jnp.concatenate([xsb_ref[rows, gw]] * nrep, axis=0) * head_block
        st_g = st_ref[g]
        y_inter = _dot(cg, st_g.astype(BF16)) * e_cum_x[:, gw]
        st_ref[g] = st_g * dec_x[:, gw] + _dot_tn(bg, xw_bf[:, gw])
        yacc_ref[rows, gw] = _dot(mix, x_blocks) + y_inter


def _ssd_layer_kernel(h_ref, g_ref, win_ref, carry0_ref, s0_ref,
                      convw_ref, convb_ref, dtb_ref, alog_ref, dskip_ref, normg_ref, wout_ref,
                      o_ref, cout_ref, sout_ref,
                      z_ref, ext_ref, st_ref, xs_ref, xsb_ref, bm_ref, cm_ref, yacc_ref,
                      *, t, q, nt, has_init):
    f = pl.program_id(0)
    i = lax.rem(f, nt)
    halo = SUBLANES
    taps = SSD_CONV_W

    @pl.when(i == 0)
    def _():
        if has_init:
            ext_ref[0:halo, :] = carry0_ref[0]
            for g in range(SSD_N_GROUPS):
                heads = s0_ref[0, g * SSD_HEADS_PER_GROUP:(g + 1) * SSD_HEADS_PER_GROUP]
                st_ref[g] = heads.reshape(SSD_GROUP_W, SSD_D_STATE).T
        else:
            ext_ref[0:halo, :] = jnp.zeros((halo, SSD_CONV_DIM), F32)
            st_ref[...] = jnp.zeros(st_ref.shape, F32)

    h = h_ref[...]
    xn = _rmsnorm_rows(h, g_ref[...]).astype(BF16)

    for c0 in range(0, SSD_CONV_DIM, CONV_BLOCK_COLS):
        cols = slice(c0, c0 + CONV_BLOCK_COLS)
        ext_ref[halo:halo + t, cols] = _dot(xn, win_ref[:, cols])
        wk = [convw_ref[k:k + 1, cols] for k in range(taps)]
        bias = convb_ref[:, cols]
        for r0 in range(0, t, CONV_BLOCK_ROWS):
            blk = ext_ref[r0:r0 + halo + CONV_BLOCK_ROWS, cols]
            acc = bias + blk[halo:, :] * wk[taps - 1]
            for d in range(1, taps):
                acc = acc + pltpu.roll(blk, d, axis=0)[halo:, :] * wk[taps - 1 - d]
            val = acc * _sigmoid(acc)
            rows = slice(r0, r0 + CONV_BLOCK_ROWS)
            if c0 < SSD_D_INNER:
                xs_ref[rows, cols] = val
                xsb_ref[rows, cols] = val.astype(BF16)
            elif c0 < SSD_D_INNER + SSD_BC_DIM:
                bm_ref[rows, c0 - SSD_D_INNER:c0 - SSD_D_INNER + CONV_BLOCK_COLS] = val.astype(BF16)
            else:
                off = c0 - SSD_D_INNER - SSD_BC_DIM
                cm_ref[rows, off:off + CONV_BLOCK_COLS] = val.astype(BF16)
    ext_ref[0:halo, :] = ext_ref[t:t + halo, :]
    z_ref[...] = _dot(xn, win_ref[:, SSD_CONV_DIM:SSD_CONV_DIM + SSD_D_INNER])
    dt_cols = slice(SSD_CONV_DIM + SSD_D_INNER, SSD_CONV_DIM + SSD_D_INNER + LANES)
    v = _dot(xn, win_ref[:, dt_cols]) + dtb_ref[...]
    dt = jnp.maximum(v, 0.0) + jnp.log1p(jnp.exp(-jnp.abs(v)))
    head_lane = lax.broadcasted_iota(jnp.int32, (1, LANES), 1) < SSD_N_HEADS
    da = dt * jnp.where(head_lane, -jnp.exp(alog_ref[...]), 0.0)

    consts = _ssd_consts(q)
    for r0 in range(0, t, q):
        _ssd_chunk(r0, q, dt[r0:r0 + q], da[r0:r0 + q], consts,
                   xs_ref, xsb_ref, bm_ref, cm_ref, st_ref, yacc_ref)

    zz = z_ref[...]
    y = (yacc_ref[...] + dskip_ref[...] * xs_ref[...]) * (zz * _sigmoid(zz))
    normed = []
    for g in range(SSD_N_GROUPS):
        gw = slice(g * SSD_GROUP_W, (g + 1) * SSD_GROUP_W)
        yg = y[:, gw]
        ms = jnp.mean(yg * yg, axis=-1, keepdims=True)
        normed.append((yg * lax.rsqrt(ms + EPS) * normg_ref[:, gw]).astype(BF16))
    o_ref[...] = h + _dot(jnp.concatenate(normed, axis=1), wout_ref[...])

    @pl.when(i == nt - 1)
    def _():
        cout_ref[0] = ext_ref[0:halo, :]
        for g in range(SSD_N_GROUPS):
            sout_ref[0, g * SSD_HEADS_PER_GROUP:(g + 1) * SSD_HEADS_PER_GROUP] = (
                st_ref[g].T.reshape(SSD_HEADS_PER_GROUP, SSD_HEAD_DIM, SSD_D_STATE))


def ssd_layer(h, n_seq, length, conv_state, ssm_state, g_mix, w):
    t = min(length, SSD_TILE)
    has_init = conv_state is not None
    state_io_shape = (SSD_N_HEADS, SSD_HEAD_DIM, SSD_D_STATE)
    if has_init:
        carry0 = jnp.concatenate(
            [jnp.zeros((n_seq, SUBLANES - (SSD_CONV_W - 1), SSD_CONV_DIM), F32), conv_state], axis=1)
        s0 = ssm_state
        seq_in3 = lambda f: (f // nt, 0, 0)
        seq_in4 = lambda f: (f // nt, 0, 0, 0)
    else:
        carry0 = jnp.zeros((1, SUBLANES, SSD_CONV_DIM), F32)
        s0 = jnp.zeros((1,) + state_io_shape, F32)
        seq_in3 = lambda f: (0, 0, 0)
        seq_in4 = lambda f: (0, 0, 0, 0)
    nt = length // t
    n_tiles = n_seq * nt
    rows = lambda f: (f, 0)
    const2 = lambda f: (0, 0)
    h_new, conv_out, st = pl.pallas_call(
        functools.partial(_ssd_layer_kernel, t=t, q=CHUNK, nt=nt, has_init=has_init),
        grid=(n_tiles,),
        in_specs=[pl.BlockSpec((t, D_MODEL), rows),
                  pl.BlockSpec((1, D_MODEL), const2),
                  _resident((D_MODEL, SSD_IN_PAD), const2),
                  pl.BlockSpec((1, SUBLANES, SSD_CONV_DIM), seq_in3),
                  pl.BlockSpec((1,) + state_io_shape, seq_in4),
                  pl.BlockSpec((SSD_CONV_W, SSD_CONV_DIM), const2),
                  pl.BlockSpec((1, SSD_CONV_DIM), const2),
                  pl.BlockSpec((1, LANES), const2),
                  pl.BlockSpec((1, LANES), const2),
                  pl.BlockSpec((1, SSD_D_INNER), const2),
                  pl.BlockSpec((1, SSD_D_INNER), const2),
                  _resident((SSD_D_INNER, D_MODEL), const2)],
        out_specs=[pl.BlockSpec((t, D_MODEL), rows),
                   pl.BlockSpec((1, SUBLANES, SSD_CONV_DIM), lambda f: (f // nt, 0, 0)),
                   pl.BlockSpec((1,) + state_io_shape, lambda f: (f // nt, 0, 0, 0))],
        out_shape=[jax.ShapeDtypeStruct((n_seq * length, D_MODEL), F32),
                   jax.ShapeDtypeStruct((n_seq, SUBLANES, SSD_CONV_DIM), F32),
                   jax.ShapeDtypeStruct((n_seq,) + state_io_shape, F32)],
        scratch_shapes=[pltpu.VMEM((t, SSD_D_INNER), F32),
                        pltpu.VMEM((t + SUBLANES, SSD_CONV_DIM), F32),
                        pltpu.VMEM((SSD_N_GROUPS, SSD_D_STATE, SSD_GROUP_W), F32),
                        pltpu.VMEM((t, SSD_D_INNER), F32),
                        pltpu.VMEM((t, SSD_D_INNER), BF16),
                        pltpu.VMEM((t, SSD_BC_DIM), BF16),
                        pltpu.VMEM((t, SSD_BC_DIM), BF16),
                        pltpu.VMEM((t, SSD_D_INNER), F32)],
        compiler_params=_cparams("arbitrary"),
        name="ssd_layer",
    )(h, g_mix.reshape(1, D_MODEL), w["w_in"], carry0, s0, w["conv_w"], w["conv_b"], w["dt_bias"],
      w["a_log"], w["d_skip_x"], w["norm_g"], w["w_out"])
    return (h_new, conv_out[:, SUBLANES - (SSD_CONV_W - 1):, :],
            st)


def _pad_lanes(v):
    return jnp.zeros((1, LANES), F32).at[0, :v.shape[0]].set(v)


def prep_ssd_weights(w_in, conv_w, conv_b, dt_bias, a_log, d_skip, norm_g, w_out):
    z_cols, xbc_cols, dt_cols = jnp.split(w_in.astype(BF16), [SSD_D_INNER, SSD_D_INNER + SSD_CONV_DIM], axis=1)
    pad_cols = jnp.zeros((D_MODEL, SSD_IN_PAD - SSD_IN_DIM), BF16)
    w_in_p = jnp.concatenate([xbc_cols, z_cols, dt_cols, pad_cols], axis=1)
    return dict(w_in=w_in_p, conv_w=conv_w, conv_b=conv_b.reshape(1, SSD_CONV_DIM),
                dt_bias=_pad_lanes(dt_bias), a_log=_pad_lanes(a_log),
                d_skip_x=jnp.repeat(d_skip, SSD_HEAD_DIM).reshape(1, SSD_D_INNER),
                norm_g=norm_g.reshape(1, SSD_D_INNER), w_out=w_out.astype(BF16))


def _dsa_in_proj_kernel(x_ref, g_ref, w_ref, q_ref, qi_ref, kw_ref, k_ref, v_ref, ki_ref,
                        kb_ref, vb_ref, kib_ref):
    proj = _dot(_rmsnorm_rows(x_ref[...], g_ref[...]).astype(BF16), w_ref[...])
    q_ref[...] = (proj[:, 0:ATT_Q_DIM] * (ATT_HEAD_DIM ** -0.5)).astype(BF16)
    qi_ref[...] = (proj[:, DSA_QI_OFF:DSA_QI_OFF + IDX_Q_DIM] * (IDX_HEAD_DIM ** -0.5)).astype(BF16)
    kw_ref[...] = proj[:, DSA_KI_OFF:DSA_KI_OFF + LANES]
    k = proj[:, DSA_K_OFF:DSA_K_OFF + ATT_KV_DIM]
    v = proj[:, DSA_V_OFF:DSA_V_OFF + ATT_KV_DIM]
    ki = proj[:, DSA_KI_OFF:DSA_KI_OFF + IDX_HEAD_DIM]
    k_ref[...] = k
    v_ref[...] = v
    ki_ref[...] = ki
    kb_ref[...] = k.astype(BF16)
    vb_ref[...] = v.astype(BF16)
    kib_ref[...] = ki.astype(BF16)


def dsa_in_proj(h, g, w):
    m, d = h.shape
    tm = min(m, 512)
    row = lambda i: (i, 0)
    widths = [(ATT_Q_DIM, BF16), (IDX_Q_DIM, BF16), (LANES, F32),
              (ATT_KV_DIM, F32), (ATT_KV_DIM, F32), (IDX_HEAD_DIM, F32),
              (ATT_KV_DIM, BF16), (ATT_KV_DIM, BF16), (IDX_HEAD_DIM, BF16)]
    return pl.pallas_call(
        _dsa_in_proj_kernel,
        grid=(m // tm,),
        in_specs=[pl.BlockSpec((tm, d), row),
                  pl.BlockSpec((1, d), lambda i: (0, 0)),
                  _resident((d, DSA_IN_PAD), lambda i: (0, 0))],
        out_specs=[pl.BlockSpec((tm, n), row) for n, _ in widths],
        out_shape=[jax.ShapeDtypeStruct((m, n), dt) for n, dt in widths],
        compiler_params=_cparams("parallel"),
        name="dsa_in_proj",
    )(h, g.reshape(1, d), w)


def _dsa_kernel(q_ref, qi_ref, kw_ref, k_ref, v_ref, ki_ref, h_ref, wout_ref, o_ref,
                key_ref, bias_ref, att_ref, *, tq, s, n_keys, q_start, topk):
    i = pl.program_id(1)
    kpos = lax.broadcasted_iota(jnp.int32, (tq, s), 1)
    qpos = q_start + i * tq + lax.broadcasted_iota(jnp.int32, (tq, s), 0)
    kchunk = jnp.where(kpos < n_keys, kpos >> 6, jnp.int32(2 ** 30))
    admissible = kchunk <= (qpos >> 6)

    if min(s, n_keys) <= topk:
        bias_ref[...] = jnp.where(admissible, 0.0, NEG)
    else:
        qi = qi_ref[...]
        wi = kw_ref[:, DSA_WI_LANE:DSA_WI_LANE + IDX_N_HEADS] * (IDX_N_HEADS ** -0.5)
        ki = ki_ref[0]
        score = jnp.zeros((tq, s), F32)
        for h in range(IDX_N_HEADS):
            d = _dot_nt(qi[:, h * IDX_HEAD_DIM:(h + 1) * IDX_HEAD_DIM], ki)
            score = score + wi[:, h:h + 1] * jnp.maximum(d, 0.0)
        score = jnp.where(admissible, score + 0.0, -jnp.inf)

        bits = pltpu.bitcast(score, jnp.int32)
        key_ref[...] = jnp.where(bits < 0, bits ^ jnp.int32(0x7FFFFFFF), bits)
        kf = jnp.float32(topk)

        def bisect(step, t):
            cand = t + jnp.left_shift(jnp.int32(1), 31 - step)
            cnt = jnp.sum(jnp.where(key_ref[...] >= cand, 1.0, 0.0), axis=1, keepdims=True)
            return jnp.where(cnt >= kf, cand, t)

        thr = lax.fori_loop(0, 32, bisect, jnp.full((tq, 1), INT_MIN, jnp.int32))

        key = key_ref[...]
        above = key > thr
        tie = key == thr
        need = kf - jnp.sum(jnp.where(above, 1.0, 0.0), axis=1, keepdims=True)
        upper = jnp.where(lax.broadcasted_iota(jnp.int32, (LANES, LANES), 0)
                          <= lax.broadcasted_iota(jnp.int32, (LANES, LANES), 1), 1.0, 0.0).astype(BF16)
        seen = jnp.zeros((tq, 1), F32)
        for blk in range(s // LANES):
            sl = slice(blk * LANES, (blk + 1) * LANES)
            tie_b = tie[:, sl]
            rank = _dot(jnp.where(tie_b, 1.0, 0.0).astype(BF16), upper) + seen
            keep = jnp.where(above[:, sl], 0.0, jnp.where(tie_b, jnp.where(rank <= need, 0.0, NEG), NEG))
            bias_ref[:, sl] = jnp.where(admissible[:, sl], keep, NEG)
            seen = rank[:, LANES - 1:LANES]

    qq = q_ref[...]
    bias = bias_ref[...]
    for g in range(ATT_N_KV_HEADS):
        heads = [qq[:, (g * ATT_REP + r) * ATT_HEAD_DIM:(g * ATT_REP + r + 1) * ATT_HEAD_DIM]
                 for r in range(ATT_REP)]
        qg = jnp.concatenate(heads, axis=0)
        kg = k_ref[0, :, g * ATT_HEAD_DIM:(g + 1) * ATT_HEAD_DIM]
        vg = v_ref[0, :, g * ATT_HEAD_DIM:(g + 1) * ATT_HEAD_DIM]
        logits = _dot_nt(qg, kg)
        logits = (logits.reshape(ATT_REP, tq, s) + bias[None]).reshape(ATT_REP * tq, s)
        m = jnp.max(logits, axis=1, keepdims=True)
        p = jnp.exp(logits - m)
        denom = jnp.sum(p, axis=1, keepdims=True)
        og = _dot(p.astype(BF16), vg) / denom
        for r in range(ATT_REP):
            hh = g * ATT_REP + r
            att_ref[:, hh * ATT_HEAD_DIM:(hh + 1) * ATT_HEAD_DIM] = og[r * tq:(r + 1) * tq].astype(BF16)

    o_ref[...] = h_ref[...] + _dot(att_ref[...], wout_ref[...])


def dsa_core(q, qi, kw, k_all, v_all, ki_all, h, w_out, n_seq, length, n_keys, past, q_off, n_rows, tq):
    s_pad = k_all.shape[1]
    s = min(s_pad, -(-(past + q_off + n_rows) // LANES) * LANES)
    nq = n_rows // tq
    topk = min(TOPK_MAX, n_keys // 4)
    rows = lambda b, i: (b * (length // tq) + q_off // tq + i, 0)
    seq = lambda b, i: (b, 0, 0)
    return pl.pallas_call(
        functools.partial(_dsa_kernel, tq=tq, s=s, n_keys=n_keys, q_start=past + q_off, topk=topk),
        grid=(n_seq, nq),
        in_specs=[pl.BlockSpec((tq, ATT_Q_DIM), rows),
                  pl.BlockSpec((tq, IDX_Q_DIM), rows),
                  pl.BlockSpec((tq, LANES), rows),
                  pl.BlockSpec((1, s, ATT_KV_DIM), seq),
                  pl.BlockSpec((1, s, ATT_KV_DIM), seq),
                  pl.BlockSpec((1, s, IDX_HEAD_DIM), seq),
                  pl.BlockSpec((tq, D_MODEL), rows),
                  _resident((ATT_Q_DIM, D_MODEL), lambda b, i: (0, 0))],
        out_specs=pl.BlockSpec((tq, D_MODEL), rows),
        out_shape=jax.ShapeDtypeStruct((n_seq * length, D_MODEL), F32),
        input_output_aliases={6: 0},
        scratch_shapes=[pltpu.VMEM((tq, s), jnp.int32),
                        pltpu.VMEM((tq, s), F32),
                        pltpu.VMEM((tq, ATT_Q_DIM), BF16)],
        compiler_params=_cparams("parallel", "arbitrary"),
        name="dsa_core",
    )(q, qi, kw, k_all, v_all, ki_all, h, w_out)


def dsa_layer(h, n_seq, length, cache, g_mix, w):
    q, qi, kw, k_new, v_new, ki_new, kb, vb, kib = dsa_in_proj(h, g_mix, w["w_in"])
    parts = ([kb.reshape(n_seq, length, ATT_KV_DIM)], [vb.reshape(n_seq, length, ATT_KV_DIM)],
             [kib.reshape(n_seq, length, IDX_HEAD_DIM)])
    past = 0
    if cache is not None:
        cache_k, cache_v, cache_ki = cache
        past = cache_k.shape[1]
        olds = (cache_k.reshape(n_seq, past, ATT_KV_DIM), cache_v.reshape(n_seq, past, ATT_KV_DIM), cache_ki)
        parts = tuple([old.astype(BF16)] + new for old, new in zip(olds, parts))
    n_keys = past + length
    s_pad = -(-n_keys // LANES) * LANES

    def keys(ps):
        if s_pad > n_keys:
            ps = ps + [jnp.zeros((n_seq, s_pad - n_keys, ps[0].shape[2]), BF16)]
        return ps[0] if len(ps) == 1 else jnp.concatenate(ps, axis=1)

    k_all, v_all, ki_all = keys(parts[0]), keys(parts[1]), keys(parts[2])
    rows_per_call = min(length, DSA_ROWS_PER_CALL)
    for q_off in range(0, length, rows_per_call):
        h = dsa_core(q, qi, kw, k_all, v_all, ki_all, h, w["w_out"], n_seq, length, n_keys, past,
                     q_off, rows_per_call, rows_per_call)
    return (h, k_new.reshape(n_seq, length, ATT_N_KV_HEADS, ATT_HEAD_DIM),
            v_new.reshape(n_seq, length, ATT_N_KV_HEADS, ATT_HEAD_DIM),
            ki_new.reshape(n_seq, length, IDX_HEAD_DIM))


def _shortconv_kernel(h_ref, g_ref, win_ref, carry0_ref, cw_ref, wout_ref,
                      o_ref, sout_ref, ext_ref, *, t, has_init):
    i = pl.program_id(1)
    halo = SUBLANES

    @pl.when(i == 0)
    def _():
        if has_init:
            ext_ref[0:halo, :] = carry0_ref[0]
        else:
            ext_ref[0:halo, :] = jnp.zeros((halo, D_MODEL), F32)

    h = h_ref[...]
    proj = _dot(_rmsnorm_rows(h, g_ref[...]).astype(BF16), win_ref[...])
    ext_ref[halo:halo + t, :] = proj[:, D_MODEL:2 * D_MODEL] * proj[:, 2 * D_MODEL:3 * D_MODEL]
    first = halo - (SC_WIDTH - 1)
    y = ext_ref[first:first + t, :] * cw_ref[0:1, :]
    for k in range(1, SC_WIDTH):
        y = y + ext_ref[first + k:first + k + t, :] * cw_ref[k:k + 1, :]
    ext_ref[0:halo, :] = ext_ref[t:t + halo, :]
    o_ref[...] = h + _dot((proj[:, 0:D_MODEL] * y).astype(BF16), wout_ref[...])

    @pl.when(i == pl.num_programs(1) - 1)
    def _():
        sout_ref[0] = ext_ref[0:halo, :]


def shortconv_layer(h, n_seq, length, conv_state, g_mix, w):
    t = min(length, SC_TILE)
    nt = length // t
    has_init = conv_state is not None
    if has_init:
        carry0 = jnp.concatenate(
            [jnp.zeros((n_seq, SUBLANES - (SC_WIDTH - 1), D_MODEL), F32), conv_state], axis=1)
        seq_map = lambda b, i: (b, 0, 0)
    else:
        carry0 = jnp.zeros((1, SUBLANES, D_MODEL), F32)
        seq_map = lambda b, i: (0, 0, 0)
    rows = lambda b, i: (b * nt + i, 0)
    const2 = lambda b, i: (0, 0)
    h_new, st = pl.pallas_call(
        functools.partial(_shortconv_kernel, t=t, has_init=has_init),
        grid=(n_seq, nt),
        in_specs=[pl.BlockSpec((t, D_MODEL), rows),
                  pl.BlockSpec((1, D_MODEL), const2),
                  _resident((D_MODEL, 3 * D_MODEL), const2),
                  pl.BlockSpec((1, SUBLANES, D_MODEL), seq_map),
                  pl.BlockSpec((SC_WIDTH, D_MODEL), const2),
                  _resident((D_MODEL, D_MODEL), const2)],
        out_specs=[pl.BlockSpec((t, D_MODEL), rows),
                   pl.BlockSpec((1, SUBLANES, D_MODEL), lambda b, i: (b, 0, 0))],
        out_shape=[jax.ShapeDtypeStruct((n_seq * length, D_MODEL), F32),
                   jax.ShapeDtypeStruct((n_seq, SUBLANES, D_MODEL), F32)],
        scratch_shapes=[pltpu.VMEM((t + SUBLANES, D_MODEL), F32)],
        compiler_params=_cparams("parallel", "arbitrary"),
        name="shortconv_layer",
    )(h, g_mix.reshape(1, D_MODEL), w["w_in"], carry0, w["conv_w"], w["w_out"])
    return h_new, st[:, SUBLANES - (SC_WIDTH - 1):, :]


def _trunk(x, states, weights, norm_mix_g, norm_ffn_g, norm_final_g, ffn_w):
    n_seq, length, _ = x.shape
    h = x.reshape(n_seq * length, D_MODEL)
    depth = len(weights)
    new_states = []
    for i in range(depth):
        kind = i % 3
        st = states[i]
        if kind == 0:
            conv_state, ssm_state = st if st is not None else (None, None)
            h, new_conv, new_ssm = ssd_layer(h, n_seq, length, conv_state, ssm_state,
                                             norm_mix_g[i], weights[i])
            new_states.append((new_conv, new_ssm))
        elif kind == 1:
            h, k_new, v_new, ki_new = dsa_layer(h, n_seq, length, st, norm_mix_g[i], weights[i])
            new_states.append((k_new, v_new, ki_new))
        else:
            h, new_conv = shortconv_layer(h, n_seq, length, st[0] if st is not None else None,
                                          norm_mix_g[i], weights[i])
            new_states.append((new_conv,))
        wg, wu, wd = ffn_w[i]
        h = ffn(h, norm_ffn_g[i], wg, wu, wd, norm_final_g, final_norm=(i == depth - 1))
    return h.reshape(n_seq, length, D_MODEL), new_states


def kernel(x_prompt, x_sample, state_l0_conv, state_l0_ssm, cache_l1_k, cache_l1_v, cache_l1_kidx, state_l2_conv, state_l3_conv, state_l3_ssm, norm_mix_g, norm_ffn_g, norm_final_g, ffn_w_gate, ffn_w_up, ffn_w_down, l0_w_in, l0_conv_w, l0_conv_b, l0_dt_bias, l0_a_log, l0_d_skip, l0_norm_g, l0_w_out, l1_w_in, l1_w_out, l2_w_in, l2_conv_w, l2_w_out, l3_w_in, l3_conv_w, l3_conv_b, l3_dt_bias, l3_a_log, l3_d_skip, l3_norm_g, l3_w_out):
    depth = norm_mix_g.shape[0]
    weights = [
        prep_ssd_weights(l0_w_in, l0_conv_w, l0_conv_b, l0_dt_bias, l0_a_log, l0_d_skip, l0_norm_g, l0_w_out),
        dict(w_in=jnp.zeros((D_MODEL, DSA_IN_PAD), BF16).at[:, :DSA_IN_DIM].set(l1_w_in.astype(BF16)),
             w_out=l1_w_out.astype(BF16)),
        dict(w_in=l2_w_in.astype(BF16), conv_w=l2_conv_w, w_out=l2_w_out.astype(BF16)),
        prep_ssd_weights(l3_w_in, l3_conv_w, l3_conv_b, l3_dt_bias, l3_a_log, l3_d_skip, l3_norm_g, l3_w_out),
    ]
    ffn_w = [(ffn_w_gate[i].astype(BF16), ffn_w_up[i].astype(BF16), ffn_w_down[i].astype(BF16))
             for i in range(depth)]
    sample_states = [(state_l0_conv, state_l0_ssm), (cache_l1_k, cache_l1_v, cache_l1_kidx),
                     (state_l2_conv,), (state_l3_conv, state_l3_ssm)]
    prompt_states = [None] * depth
    y_p, new_p = _trunk(x_prompt, prompt_states, weights, norm_mix_g, norm_ffn_g, norm_final_g, ffn_w)
    y_s, new_s = _trunk(x_sample, sample_states, weights, norm_mix_g, norm_ffn_g, norm_final_g, ffn_w)
    (p_l0_conv, p_l0_ssm), (p_l1_k, p_l1_v, p_l1_kidx), (p_l2_conv,), (p_l3_conv, p_l3_ssm) = new_p
    (s_l0_conv, s_l0_ssm), (s_l1_k, s_l1_v, s_l1_kidx), (s_l2_conv,), (s_l3_conv, s_l3_ssm) = new_s
    return (y_p, y_s,
            p_l0_conv, p_l0_ssm, s_l0_conv, s_l0_ssm,
            p_l1_k, p_l1_v, p_l1_kidx, s_l1_k, s_l1_v, s_l1_kidx,
            p_l2_conv, s_l2_conv,
            p_l3_conv, p_l3_ssm, s_l3_conv, s_l3_ssm)
```

```python
import functools

import jax
import jax.numpy as jnp
from jax import lax
from jax.experimental import pallas as pl
from jax.experimental.pallas import tpu as pltpu

F32 = jnp.float32
BF16 = jnp.bfloat16

EPS = 1e-6
CHUNK = 64

D_MODEL = 1024
FFN_HIDDEN = 2816

SSD_D_INNER = 2048
SSD_HEAD_DIM = 64
SSD_N_HEADS = 32
SSD_N_GROUPS = 8
SSD_HEADS_PER_GROUP = 4
SSD_D_STATE = 128
SSD_CONV_W = 4
SSD_BC_DIM = SSD_N_GROUPS * SSD_D_STATE
SSD_CONV_DIM = SSD_D_INNER + 2 * SSD_BC_DIM
SSD_IN_DIM = SSD_D_INNER + SSD_CONV_DIM + SSD_N_HEADS
SSD_IN_PAD = 6400
SSD_GROUP_W = SSD_HEADS_PER_GROUP * SSD_HEAD_DIM
SSD_TILE = 256
CONV_BLOCK_ROWS = 64
CONV_BLOCK_COLS = 256

ATT_N_HEADS = 16
ATT_N_KV_HEADS = 4
ATT_REP = ATT_N_HEADS // ATT_N_KV_HEADS
ATT_HEAD_DIM = 64
ATT_Q_DIM = 1024
ATT_KV_DIM = 256
IDX_N_HEADS = 8
IDX_HEAD_DIM = 64
IDX_Q_DIM = 512
DSA_IN_DIM = 2120
DSA_IN_PAD = 2304
DSA_K_OFF = 1024
DSA_V_OFF = 1280
DSA_QI_OFF = 1536
DSA_KI_OFF = 2048
DSA_WI_LANE = 64
DSA_ROWS_PER_CALL = 256
TOPK_MAX = 256

SC_WIDTH = 3
SC_TILE = 512

LANES = 128
SUBLANES = 8
VMEM_LIMIT = 56 * 1024 * 1024
NEG = -1e30
INT_MIN = -(2 ** 31)


def _cparams(*sem):
    return pltpu.CompilerParams(dimension_semantics=sem, vmem_limit_bytes=VMEM_LIMIT)


def _resident(shape, index_map):
    return pl.BlockSpec(shape, index_map, pipeline_mode=pl.Buffered(1))


def _sigmoid(v):
    return 1.0 / (1.0 + jnp.exp(-v))


def _dot(a, b):
    return jnp.dot(a, b, preferred_element_type=F32)


def _dot_nt(a, b, precision=None):
    return lax.dot_general(a, b, (((1,), (1,)), ((), ())), precision=precision,
                           preferred_element_type=F32)


def _dot_tn(a, b):
    return lax.dot_general(a, b, (((0,), (0,)), ((), ())), preferred_element_type=F32)


def _rmsnorm_rows(x, g):
    ms = jnp.mean(x * x, axis=-1, keepdims=True)
    return x * lax.rsqrt(ms + EPS) * g


def _ffn_kernel(h_ref, g_ref, wg_ref, wu_ref, wd_ref, gf_ref, o_ref, *, final_norm):
    h = h_ref[...]
    xn = _rmsnorm_rows(h, g_ref[...]).astype(BF16)
    a = _dot(xn, wg_ref[...])
    b = _dot(xn, wu_ref[...])
    t = (a * _sigmoid(a) * b).astype(BF16)
    out = h + _dot(t, wd_ref[...])
    if final_norm:
        out = _rmsnorm_rows(out, gf_ref[...])
    o_ref[...] = out


def ffn(h, g, wg, wu, wd, gf, layer, final_norm):
    m, d = h.shape
    f = wg.shape[2]
    tm = min(m, 512)
    return pl.pallas_call(
        functools.partial(_ffn_kernel, final_norm=final_norm),
        grid=(m // tm,),
        in_specs=[pl.BlockSpec((tm, d), lambda i: (i, 0)),
                  pl.BlockSpec((1, d), lambda i: (0, 0)),
                  _resident((None, d, f), lambda i: (layer, 0, 0)),
                  _resident((None, d, f), lambda i: (layer, 0, 0)),
                  _resident((None, f, d), lambda i: (layer, 0, 0)),
                  pl.BlockSpec((1, d), lambda i: (0, 0))],
        out_specs=pl.BlockSpec((tm, d), lambda i: (i, 0)),
        out_shape=jax.ShapeDtypeStruct((m, d), F32),
        compiler_params=_cparams("parallel"),
        name="ffn",
    )(h, g.reshape(1, d), wg, wu, wd, gf.reshape(1, d))


def _expand_heads(v):
    r = v.shape[0]
    low_half = lax.broadcasted_iota(jnp.int32, (r, LANES), 1) < SSD_HEAD_DIM
    cols = []
    for p in range(SSD_N_HEADS // 2):
        lo = jnp.broadcast_to(v[:, 2 * p:2 * p + 1], (r, LANES))
        hi = jnp.broadcast_to(v[:, 2 * p + 1:2 * p + 2], (r, LANES))
        cols.append(jnp.where(low_half, lo, hi))
    return jnp.concatenate(cols, axis=1)


def _ssd_consts(q):
    row = lax.broadcasted_iota(jnp.int32, (q, q), 0)
    col = lax.broadcasted_iota(jnp.int32, (q, q), 1)
    tril = jnp.where(row >= col, 1.0, 0.0).astype(F32)
    eye = jnp.where(lax.broadcasted_iota(jnp.int32, (LANES, LANES), 0)
                    == lax.broadcasted_iota(jnp.int32, (LANES, LANES), 1), 1.0, 0.0).astype(F32)
    i_w = lax.broadcasted_iota(jnp.int32, (q, SSD_GROUP_W), 0)
    j_w = lax.broadcasted_iota(jnp.int32, (q, SSD_GROUP_W), 1) & (q - 1)
    causal_w = i_w >= j_w
    r_row = lax.broadcasted_iota(jnp.int32, (SSD_GROUP_W, SSD_GROUP_W), 0) // q
    r_col = lax.broadcasted_iota(jnp.int32, (SSD_GROUP_W, SSD_GROUP_W), 1) // SSD_HEAD_DIM
    head_block = jnp.where(r_row == r_col, 1.0, 0.0).astype(BF16)
    return tril, eye, causal_w, head_block


def _ssd_chunk(r0, q, dt, da, consts, xs_ref, xsb_ref, bm_ref, cm_ref, st_ref, yacc_ref):
    tril, eye, causal_w, head_block = consts
    rows = slice(r0, r0 + q)
    nrep = SSD_HEADS_PER_GROUP
    cum = jnp.dot(tril, da, precision=lax.Precision.HIGHEST, preferred_element_type=F32)
    cum_t = _dot_nt(eye, cum, precision=lax.Precision.HIGHEST)
    dt_t = _dot_nt(eye, dt, precision=lax.Precision.HIGHEST)
    cum_last = cum[q - 1:q, :]
    cum_x = _expand_heads(cum)
    e_cum_x = jnp.exp(cum_x)
    w_end_x = _expand_heads(jnp.exp(cum_last - cum) * dt)
    dec_x = _expand_heads(jnp.exp(cum_last))
    xw_bf = (xs_ref[rows, :] * w_end_x).astype(BF16)

    for g in range(SSD_N_GROUPS):
        gl = slice(g * SSD_D_STATE, (g + 1) * SSD_D_STATE)
        gw = slice(g * SSD_GROUP_W, (g + 1) * SSD_GROUP_W)
        heads = range(g * nrep, (g + 1) * nrep)
        bg = bm_ref[rows, gl]
        cg = cm_ref[rows, gl]
        cum_src = jnp.concatenate([cum_t[h:h + 1, :] for h in heads], axis=1)
        dt_src = jnp.concatenate([dt_t[h:h + 1, :] for h in heads], axis=1)
        cb = _dot_nt(cg, jnp.concatenate([bg] * nrep, axis=0))
        decay = jnp.exp(jnp.where(causal_w, cum_x[:, gw] - cum_src, NEG))
        mix = (cb * decay * dt_src).astype(BF16)
        x_blocks = jnp.concatenate([xsb_ref[rows, gw]] * nrep, axis=0) * head_block
        st_g = st_ref[g]
        y_inter = _dot(cg, st_g.astype(BF16)) * e_cum_x[:, gw]
        st_ref[g] = st_g * dec_x[:, gw] + _dot_tn(bg, xw_bf[:, gw])
        yacc_ref[rows, gw] = _dot(mix, x_blocks) + y_inter


def _ssd_layer_kernel(h_ref, g_ref, win_ref, carry0_ref, s0_ref,
                      convw_ref, convb_ref, dtb_ref, alog_ref, dskip_ref, normg_ref, wout_ref,
                      o_ref, cout_ref, sout_ref,
                      z_ref, ext_ref, st_ref, xs_ref, xsb_ref, bm_ref, cm_ref, yacc_ref,
                      *, t, q, nt, has_init):
    f = pl.program_id(0)
    i = lax.rem(f, nt)
    halo = SUBLANES
    taps = SSD_CONV_W
    n_blocks = SSD_CONV_DIM // CONV_BLOCK_COLS
    dt_cols = slice(SSD_CONV_DIM + SSD_D_INNER, SSD_CONV_DIM + SSD_D_INNER + LANES)

    @pl.when(i == 0)
    def _():
        if has_init:
            for b in range(n_blocks):
                ext_ref[b, 0:halo, :] = carry0_ref[0, :, b * CONV_BLOCK_COLS:(b + 1) * CONV_BLOCK_COLS]
            for g in range(SSD_N_GROUPS):
                heads = s0_ref[0, g * SSD_HEADS_PER_GROUP:(g + 1) * SSD_HEADS_PER_GROUP]
                st_ref[g] = heads.reshape(SSD_GROUP_W, SSD_D_STATE).T
        else:
            ext_ref[:, 0:halo, :] = jnp.zeros((n_blocks, halo, CONV_BLOCK_COLS), F32)
            st_ref[...] = jnp.zeros(st_ref.shape, F32)

    h = h_ref[...]
    xn = _rmsnorm_rows(h, g_ref[...]).astype(BF16)

    for b in range(n_blocks):
        c0 = b * CONV_BLOCK_COLS
        cols = slice(c0, c0 + CONV_BLOCK_COLS)
        ext_ref[b, halo:halo + t, :] = _dot(xn, win_ref[:, cols])
        wk = [convw_ref[k:k + 1, cols] for k in range(taps)]
        bias = convb_ref[:, cols]
        for r0 in range(0, t, CONV_BLOCK_ROWS):
            blk = ext_ref[b, r0:r0 + halo + CONV_BLOCK_ROWS, :]
            acc = bias + blk[halo:, :] * wk[taps - 1]
            for d in range(1, taps):
                acc = acc + pltpu.roll(blk, d, axis=0)[halo:, :] * wk[taps - 1 - d]
            val = acc * _sigmoid(acc)
            rows = slice(r0, r0 + CONV_BLOCK_ROWS)
            if c0 < SSD_D_INNER:
                xs_ref[rows, cols] = val
                xsb_ref[rows, cols] = val.astype(BF16)
            elif c0 < SSD_D_INNER + SSD_BC_DIM:
                bm_ref[rows, c0 - SSD_D_INNER:c0 - SSD_D_INNER + CONV_BLOCK_COLS] = val.astype(BF16)
            else:
                off = c0 - SSD_D_INNER - SSD_BC_DIM
                cm_ref[rows, off:off + CONV_BLOCK_COLS] = val.astype(BF16)
        ext_ref[b, 0:halo, :] = ext_ref[b, t:t + halo, :]
    z_ref[...] = _dot(xn, win_ref[:, SSD_CONV_DIM:SSD_CONV_DIM + SSD_D_INNER])
    v = _dot(xn, win_ref[:, dt_cols]) + dtb_ref[...]
    dt = jnp.maximum(v, 0.0) + jnp.log1p(jnp.exp(-jnp.abs(v)))
    head_lane = lax.broadcasted_iota(jnp.int32, (1, LANES), 1) < SSD_N_HEADS
    da = dt * jnp.where(head_lane, -jnp.exp(alog_ref[...]), 0.0)

    consts = _ssd_consts(q)
    for r0 in range(0, t, q):
        _ssd_chunk(r0, q, dt[r0:r0 + q], da[r0:r0 + q], consts,
                   xs_ref, xsb_ref, bm_ref, cm_ref, st_ref, yacc_ref)

    zz = z_ref[...]
    y = (yacc_ref[...] + dskip_ref[...] * xs_ref[...]) * (zz * _sigmoid(zz))
    normed = []
    for g in range(SSD_N_GROUPS):
        gw = slice(g * SSD_GROUP_W, (g + 1) * SSD_GROUP_W)
        yg = y[:, gw]
        ms = jnp.mean(yg * yg, axis=-1, keepdims=True)
        normed.append((yg * lax.rsqrt(ms + EPS) * normg_ref[:, gw]).astype(BF16))
    o_ref[...] = h + _dot(jnp.concatenate(normed, axis=1), wout_ref[...])

    @pl.when(i == nt - 1)
    def _():
        for b in range(n_blocks):
            cout_ref[0, :, b * CONV_BLOCK_COLS:(b + 1) * CONV_BLOCK_COLS] = ext_ref[b, 0:halo, :]
        for g in range(SSD_N_GROUPS):
            sout_ref[0, g * SSD_HEADS_PER_GROUP:(g + 1) * SSD_HEADS_PER_GROUP] = (
                st_ref[g].T.reshape(SSD_HEADS_PER_GROUP, SSD_HEAD_DIM, SSD_D_STATE))


def ssd_layer(h, n_seq, length, conv_state, ssm_state, g_mix, w):
    t = min(length, SSD_TILE)
    has_init = conv_state is not None
    state_io_shape = (SSD_N_HEADS, SSD_HEAD_DIM, SSD_D_STATE)
    if has_init:
        carry0 = jnp.concatenate(
            [jnp.zeros((n_seq, SUBLANES - (SSD_CONV_W - 1), SSD_CONV_DIM), F32), conv_state], axis=1)
        s0 = ssm_state
        seq_in3 = lambda f: (f // nt, 0, 0)
        seq_in4 = lambda f: (f // nt, 0, 0, 0)
    else:
        carry0 = jnp.zeros((1, SUBLANES, SSD_CONV_DIM), F32)
        s0 = jnp.zeros((1,) + state_io_shape, F32)
        seq_in3 = lambda f: (0, 0, 0)
        seq_in4 = lambda f: (0, 0, 0, 0)
    nt = length // t
    n_tiles = n_seq * nt
    rows = lambda f: (f, 0)
    const2 = lambda f: (0, 0)
    h_new, conv_out, st = pl.pallas_call(
        functools.partial(_ssd_layer_kernel, t=t, q=CHUNK, nt=nt, has_init=has_init),
        grid=(n_tiles,),
        in_specs=[pl.BlockSpec((t, D_MODEL), rows),
                  pl.BlockSpec((1, D_MODEL), const2),
                  _resident((D_MODEL, SSD_IN_PAD), const2),
                  pl.BlockSpec((1, SUBLANES, SSD_CONV_DIM), seq_in3),
                  pl.BlockSpec((1,) + state_io_shape, seq_in4),
                  pl.BlockSpec((SSD_CONV_W, SSD_CONV_DIM), const2),
                  pl.BlockSpec((1, SSD_CONV_DIM), const2),
                  pl.BlockSpec((1, LANES), const2),
                  pl.BlockSpec((1, LANES), const2),
                  pl.BlockSpec((1, SSD_D_INNER), const2),
                  pl.BlockSpec((1, SSD_D_INNER), const2),
                  _resident((SSD_D_INNER, D_MODEL), const2)],
        out_specs=[pl.BlockSpec((t, D_MODEL), rows),
                   pl.BlockSpec((1, SUBLANES, SSD_CONV_DIM), lambda f: (f // nt, 0, 0)),
                   pl.BlockSpec((1,) + state_io_shape, lambda f: (f // nt, 0, 0, 0))],
        out_shape=[jax.ShapeDtypeStruct((n_seq * length, D_MODEL), F32),
                   jax.ShapeDtypeStruct((n_seq, SUBLANES, SSD_CONV_DIM), F32),
                   jax.ShapeDtypeStruct((n_seq,) + state_io_shape, F32)],
        scratch_shapes=[pltpu.VMEM((t, SSD_D_INNER), F32),
                        pltpu.VMEM((SSD_CONV_DIM // CONV_BLOCK_COLS, t + SUBLANES, CONV_BLOCK_COLS), F32),
                        pltpu.VMEM((SSD_N_GROUPS, SSD_D_STATE, SSD_GROUP_W), F32),
                        pltpu.VMEM((t, SSD_D_INNER), F32),
                        pltpu.VMEM((t, SSD_D_INNER), BF16),
                        pltpu.VMEM((t, SSD_BC_DIM), BF16),
                        pltpu.VMEM((t, SSD_BC_DIM), BF16),
                        pltpu.VMEM((t, SSD_D_INNER), F32)],
        compiler_params=_cparams("arbitrary"),
        name="ssd_layer",
    )(h, g_mix.reshape(1, D_MODEL), w["w_in"], carry0, s0, w["conv_w"], w["conv_b"], w["dt_bias"],
      w["a_log"], w["d_skip_x"], w["norm_g"], w["w_out"])
    return (h_new, conv_out[:, SUBLANES - (SSD_CONV_W - 1):, :],
            st)


def _pad_lanes(v):
    return jnp.zeros((1, LANES), F32).at[0, :v.shape[0]].set(v)


def prep_ssd_weights(w_in, conv_w, conv_b, dt_bias, a_log, d_skip, norm_g, w_out):
    z_cols, xbc_cols, dt_cols = jnp.split(w_in.astype(BF16), [SSD_D_INNER, SSD_D_INNER + SSD_CONV_DIM], axis=1)
    pad_cols = jnp.zeros((D_MODEL, SSD_IN_PAD - SSD_IN_DIM), BF16)
    w_in_p = jnp.concatenate([xbc_cols, z_cols, dt_cols, pad_cols], axis=1)
    return dict(w_in=w_in_p, conv_w=conv_w, conv_b=conv_b.reshape(1, SSD_CONV_DIM),
                dt_bias=_pad_lanes(dt_bias), a_log=_pad_lanes(a_log),
                d_skip_x=jnp.repeat(d_skip, SSD_HEAD_DIM).reshape(1, SSD_D_INNER),
                norm_g=norm_g.reshape(1, SSD_D_INNER), w_out=w_out.astype(BF16))


def _dsa_in_proj_kernel(x_ref, g_ref, w_ref, q_ref, qi_ref, kw_ref, k_ref, v_ref, ki_ref,
                        kb_ref, vb_ref, kib_ref):
    proj = _dot(_rmsnorm_rows(x_ref[...], g_ref[...]).astype(BF16), w_ref[...])
    q_ref[...] = (proj[:, 0:ATT_Q_DIM] * (ATT_HEAD_DIM ** -0.5)).astype(BF16)
    qi_ref[...] = (proj[:, DSA_QI_OFF:DSA_QI_OFF + IDX_Q_DIM] * (IDX_HEAD_DIM ** -0.5)).astype(BF16)
    kw_ref[...] = proj[:, DSA_KI_OFF:DSA_KI_OFF + LANES]
    k = proj[:, DSA_K_OFF:DSA_K_OFF + ATT_KV_DIM]
    v = proj[:, DSA_V_OFF:DSA_V_OFF + ATT_KV_DIM]
    ki = proj[:, DSA_KI_OFF:DSA_KI_OFF + IDX_HEAD_DIM]
    k_ref[...] = k
    v_ref[...] = v
    ki_ref[...] = ki
    kb_ref[...] = k.astype(BF16)
    vb_ref[...] = v.astype(BF16)
    kib_ref[...] = ki.astype(BF16)


def dsa_in_proj(h, g, w):
    m, d = h.shape
    tm = min(m, 512)
    row = lambda i: (i, 0)
    widths = [(ATT_Q_DIM, BF16), (IDX_Q_DIM, BF16), (LANES, F32),
              (ATT_KV_DIM, F32), (ATT_KV_DIM, F32), (IDX_HEAD_DIM, F32),
              (ATT_KV_DIM, BF16), (ATT_KV_DIM, BF16), (IDX_HEAD_DIM, BF16)]
    return pl.pallas_call(
        _dsa_in_proj_kernel,
        grid=(m // tm,),
        in_specs=[pl.BlockSpec((tm, d), row),
                  pl.BlockSpec((1, d), lambda i: (0, 0)),
                  _resident((d, DSA_IN_PAD), lambda i: (0, 0))],
        out_specs=[pl.BlockSpec((tm, n), row) for n, _ in widths],
        out_shape=[jax.ShapeDtypeStruct((m, n), dt) for n, dt in widths],
        compiler_params=_cparams("parallel"),
        name="dsa_in_proj",
    )(h, g.reshape(1, d), w)


def _dsa_kernel(q_ref, qi_ref, kw_ref, k_ref, v_ref, ki_ref, h_ref, wout_ref, o_ref,
                key_ref, bias_ref, att_ref, *, tq, s, n_keys, q_start, topk):
    i = pl.program_id(1)
    kpos = lax.broadcasted_iota(jnp.int32, (tq, s), 1)
    qpos = q_start + i * tq + lax.broadcasted_iota(jnp.int32, (tq, s), 0)
    kchunk = jnp.where(kpos < n_keys, kpos >> 6, jnp.int32(2 ** 30))
    admissible = kchunk <= (qpos >> 6)

    if min(s, n_keys) <= topk:
        bias_ref[...] = jnp.where(admissible, 0.0, NEG)
    else:
        qi = qi_ref[...]
        wi = kw_ref[:, DSA_WI_LANE:DSA_WI_LANE + IDX_N_HEADS] * (IDX_N_HEADS ** -0.5)
        ki = ki_ref[0]
        score = jnp.zeros((tq, s), F32)
        for h in range(IDX_N_HEADS):
            d = _dot_nt(qi[:, h * IDX_HEAD_DIM:(h + 1) * IDX_HEAD_DIM], ki)
            score = score + wi[:, h:h + 1] * jnp.maximum(d, 0.0)
        score = jnp.where(admissible, score + 0.0, -jnp.inf)

        bits = pltpu.bitcast(score, jnp.int32)
        key_ref[...] = jnp.where(bits < 0, bits ^ jnp.int32(0x7FFFFFFF), bits)
        kf = jnp.float32(topk)

        def bisect(step, t):
            cand = t + jnp.left_shift(jnp.int32(1), 31 - step)
            cnt = jnp.sum(jnp.where(key_ref[...] >= cand, 1.0, 0.0), axis=1, keepdims=True)
            return jnp.where(cnt >= kf, cand, t)

        thr = lax.fori_loop(0, 32, bisect, jnp.full((tq, 1), INT_MIN, jnp.int32))

        key = key_ref[...]
        above = key > thr
        tie = key == thr
        need = kf - jnp.sum(jnp.where(above, 1.0, 0.0), axis=1, keepdims=True)
        upper = jnp.where(lax.broadcasted_iota(jnp.int32, (LANES, LANES), 0)
                          <= lax.broadcasted_iota(jnp.int32, (LANES, LANES), 1), 1.0, 0.0).astype(BF16)
        seen = jnp.zeros((tq, 1), F32)
        for blk in range(s // LANES):
            sl = slice(blk * LANES, (blk + 1) * LANES)
            tie_b = tie[:, sl]
            rank = _dot(jnp.where(tie_b, 1.0, 0.0).astype(BF16), upper) + seen
            keep = jnp.where(above[:, sl], 0.0, jnp.where(tie_b, jnp.where(rank <= need, 0.0, NEG), NEG))
            bias_ref[:, sl] = jnp.where(admissible[:, sl], keep, NEG)
            seen = rank[:, LANES - 1:LANES]

    qq = q_ref[...]
    bias = bias_ref[...]
    for g in range(ATT_N_KV_HEADS):
        heads = [qq[:, (g * ATT_REP + r) * ATT_HEAD_DIM:(g * ATT_REP + r + 1) * ATT_HEAD_DIM]
                 for r in range(ATT_REP)]
        qg = jnp.concatenate(heads, axis=0)
        kg = k_ref[0, :, g * ATT_HEAD_DIM:(g + 1) * ATT_HEAD_DIM]
        vg = v_ref[0, :, g * ATT_HEAD_DIM:(g + 1) * ATT_HEAD_DIM]
        logits = _dot_nt(qg, kg)
        logits = (logits.reshape(ATT_REP, tq, s) + bias[None]).reshape(ATT_REP * tq, s)
        m = jnp.max(logits, axis=1, keepdims=True)
        p = jnp.exp(logits - m)
        denom = jnp.sum(p, axis=1, keepdims=True)
        og = _dot(p.astype(BF16), vg) / denom
        for r in range(ATT_REP):
            hh = g * ATT_REP + r
            att_ref[:, hh * ATT_HEAD_DIM:(hh + 1) * ATT_HEAD_DIM] = og[r * tq:(r + 1) * tq].astype(BF16)

    o_ref[...] = h_ref[...] + _dot(att_ref[...], wout_ref[...])


def dsa_core(q, qi, kw, k_all, v_all, ki_all, h, w_out, n_seq, length, n_keys, past, q_off, n_rows, tq):
    s_pad = k_all.shape[1]
    s = min(s_pad, -(-(past + q_off + n_rows) // LANES) * LANES)
    nq = n_rows // tq
    topk = min(TOPK_MAX, n_keys // 4)
    rows = lambda b, i: (b * (length // tq) + q_off // tq + i, 0)
    seq = lambda b, i: (b, 0, 0)
    return pl.pallas_call(
        functools.partial(_dsa_kernel, tq=tq, s=s, n_keys=n_keys, q_start=past + q_off, topk=topk),
        grid=(n_seq, nq),
        in_specs=[pl.BlockSpec((tq, ATT_Q_DIM), rows),
                  pl.BlockSpec((tq, IDX_Q_DIM), rows),
                  pl.BlockSpec((tq, LANES), rows),
                  pl.BlockSpec((1, s, ATT_KV_DIM), seq),
                  pl.BlockSpec((1, s, ATT_KV_DIM), seq),
                  pl.BlockSpec((1, s, IDX_HEAD_DIM), seq),
                  pl.BlockSpec((tq, D_MODEL), rows),
                  _resident((ATT_Q_DIM, D_MODEL), lambda b, i: (0, 0))],
        out_specs=pl.BlockSpec((tq, D_MODEL), rows),
        out_shape=jax.ShapeDtypeStruct((n_seq * length, D_MODEL), F32),
        input_output_aliases={6: 0},
        scratch_shapes=[pltpu.VMEM((tq, s), jnp.int32),
                        pltpu.VMEM((tq, s), F32),
                        pltpu.VMEM((tq, ATT_Q_DIM), BF16)],
        compiler_params=_cparams("parallel", "arbitrary"),
        name="dsa_core",
    )(q, qi, kw, k_all, v_all, ki_all, h, w_out)


def dsa_layer(h, n_seq, length, cache, g_mix, w):
    q, qi, kw, k_new, v_new, ki_new, kb, vb, kib = dsa_in_proj(h, g_mix, w["w_in"])
    parts = ([kb.reshape(n_seq, length, ATT_KV_DIM)], [vb.reshape(n_seq, length, ATT_KV_DIM)],
             [kib.reshape(n_seq, length, IDX_HEAD_DIM)])
    past = 0
    if cache is not None:
        cache_k, cache_v, cache_ki = cache
        past = cache_k.shape[1]
        olds = (cache_k.reshape(n_seq, past, ATT_KV_DIM), cache_v.reshape(n_seq, past, ATT_KV_DIM), cache_ki)
        parts = tuple([old.astype(BF16)] + new for old, new in zip(olds, parts))
    n_keys = past + length
    s_pad = -(-n_keys // LANES) * LANES

    def keys(ps):
        if s_pad > n_keys:
            ps = ps + [jnp.zeros((n_seq, s_pad - n_keys, ps[0].shape[2]), BF16)]
        return ps[0] if len(ps) == 1 else jnp.concatenate(ps, axis=1)

    k_all, v_all, ki_all = keys(parts[0]), keys(parts[1]), keys(parts[2])
    rows_per_call = min(length, DSA_ROWS_PER_CALL)
    for q_off in range(0, length, rows_per_call):
        h = dsa_core(q, qi, kw, k_all, v_all, ki_all, h, w["w_out"], n_seq, length, n_keys, past,
                     q_off, rows_per_call, rows_per_call)
    return (h, k_new.reshape(n_seq, length, ATT_N_KV_HEADS, ATT_HEAD_DIM),
            v_new.reshape(n_seq, length, ATT_N_KV_HEADS, ATT_HEAD_DIM),
            ki_new.reshape(n_seq, length, IDX_HEAD_DIM))


def _shortconv_kernel(h_ref, g_ref, win_ref, carry0_ref, cw_ref, wout_ref,
                      o_ref, sout_ref, ext_ref, *, t, has_init):
    i = pl.program_id(1)
    halo = SUBLANES

    @pl.when(i == 0)
    def _():
        if has_init:
            ext_ref[0:halo, :] = carry0_ref[0]
        else:
            ext_ref[0:halo, :] = jnp.zeros((halo, D_MODEL), F32)

    h = h_ref[...]
    proj = _dot(_rmsnorm_rows(h, g_ref[...]).astype(BF16), win_ref[...])
    ext_ref[halo:halo + t, :] = proj[:, D_MODEL:2 * D_MODEL] * proj[:, 2 * D_MODEL:3 * D_MODEL]
    first = halo - (SC_WIDTH - 1)
    y = ext_ref[first:first + t, :] * cw_ref[0:1, :]
    for k in range(1, SC_WIDTH):
        y = y + ext_ref[first + k:first + k + t, :] * cw_ref[k:k + 1, :]
    ext_ref[0:halo, :] = ext_ref[t:t + halo, :]
    o_ref[...] = h + _dot((proj[:, 0:D_MODEL] * y).astype(BF16), wout_ref[...])

    @pl.when(i == pl.num_programs(1) - 1)
    def _():
        sout_ref[0] = ext_ref[0:halo, :]


def shortconv_layer(h, n_seq, length, conv_state, g_mix, w):
    t = min(length, SC_TILE)
    nt = length // t
    has_init = conv_state is not None
    if has_init:
        carry0 = jnp.concatenate(
            [jnp.zeros((n_seq, SUBLANES - (SC_WIDTH - 1), D_MODEL), F32), conv_state], axis=1)
        seq_map = lambda b, i: (b, 0, 0)
    else:
        carry0 = jnp.zeros((1, SUBLANES, D_MODEL), F32)
        seq_map = lambda b, i: (0, 0, 0)
    rows = lambda b, i: (b * nt + i, 0)
    const2 = lambda b, i: (0, 0)
    h_new, st = pl.pallas_call(
        functools.partial(_shortconv_kernel, t=t, has_init=has_init),
        grid=(n_seq, nt),
        in_specs=[pl.BlockSpec((t, D_MODEL), rows),
                  pl.BlockSpec((1, D_MODEL), const2),
                  _resident((D_MODEL, 3 * D_MODEL), const2),
                  pl.BlockSpec((1, SUBLANES, D_MODEL), seq_map),
                  pl.BlockSpec((SC_WIDTH, D_MODEL), const2),
                  _resident((D_MODEL, D_MODEL), const2)],
        out_specs=[pl.BlockSpec((t, D_MODEL), rows),
                   pl.BlockSpec((1, SUBLANES, D_MODEL), lambda b, i: (b, 0, 0))],
        out_shape=[jax.ShapeDtypeStruct((n_seq * length, D_MODEL), F32),
                   jax.ShapeDtypeStruct((n_seq, SUBLANES, D_MODEL), F32)],
        scratch_shapes=[pltpu.VMEM((t + SUBLANES, D_MODEL), F32)],
        compiler_params=_cparams("parallel", "arbitrary"),
        name="shortconv_layer",
    )(h, g_mix.reshape(1, D_MODEL), w["w_in"], carry0, w["conv_w"], w["w_out"])
    return h_new, st[:, SUBLANES - (SC_WIDTH - 1):, :]


def _trunk(x, states, weights, norm_mix_g, norm_ffn_g, norm_final_g, ffn_w):
    n_seq, length, _ = x.shape
    h = x.reshape(n_seq * length, D_MODEL)
    depth = len(weights)
    new_states = []
    for i in range(depth):
        kind = i % 3
        st = states[i]
        if kind == 0:
            conv_state, ssm_state = st if st is not None else (None, None)
            h, new_conv, new_ssm = ssd_layer(h, n_seq, length, conv_state, ssm_state,
                                             norm_mix_g[i], weights[i])
            new_states.append((new_conv, new_ssm))
        elif kind == 1:
            h, k_new, v_new, ki_new = dsa_layer(h, n_seq, length, st, norm_mix_g[i], weights[i])
            new_states.append((k_new, v_new, ki_new))
        else:
            h, new_conv = shortconv_layer(h, n_seq, length, st[0] if st is not None else None,
                                          norm_mix_g[i], weights[i])
            new_states.append((new_conv,))
        wg, wu, wd = ffn_w
        h = ffn(h, norm_ffn_g[i], wg, wu, wd, norm_final_g, layer=i, final_norm=(i == depth - 1))
    return h.reshape(n_seq, length, D_MODEL), new_states


def kernel(x_prompt, x_sample, state_l0_conv, state_l0_ssm, cache_l1_k, cache_l1_v, cache_l1_kidx, state_l2_conv, state_l3_conv, state_l3_ssm, norm_mix_g, norm_ffn_g, norm_final_g, ffn_w_gate, ffn_w_up, ffn_w_down, l0_w_in, l0_conv_w, l0_conv_b, l0_dt_bias, l0_a_log, l0_d_skip, l0_norm_g, l0_w_out, l1_w_in, l1_w_out, l2_w_in, l2_conv_w, l2_w_out, l3_w_in, l3_conv_w, l3_conv_b, l3_dt_bias, l3_a_log, l3_d_skip, l3_norm_g, l3_w_out):
    depth = norm_mix_g.shape[0]
    weights = [
        prep_ssd_weights(l0_w_in, l0_conv_w, l0_conv_b, l0_dt_bias, l0_a_log, l0_d_skip, l0_norm_g, l0_w_out),
        dict(w_in=jnp.zeros((D_MODEL, DSA_IN_PAD), BF16).at[:, :DSA_IN_DIM].set(l1_w_in.astype(BF16)),
             w_out=l1_w_out.astype(BF16)),
        dict(w_in=l2_w_in.astype(BF16), conv_w=l2_conv_w, w_out=l2_w_out.astype(BF16)),
        prep_ssd_weights(l3_w_in, l3_conv_w, l3_conv_b, l3_dt_bias, l3_a_log, l3_d_skip, l3_norm_g, l3_w_out),
    ]
    ffn_w = (ffn_w_gate.astype(BF16), ffn_w_up.astype(BF16), ffn_w_down.astype(BF16))
    sample_states = [(state_l0_conv, state_l0_ssm), (cache_l1_k, cache_l1_v, cache_l1_kidx),
                     (state_l2_conv,), (state_l3_conv, state_l3_ssm)]
    prompt_states = [None] * depth
    y_p, new_p = _trunk(x_prompt, prompt_states, weights, norm_mix_g, norm_ffn_g, norm_final_g, ffn_w)
    y_s, new_s = _trunk(x_sample, sample_states, weights, norm_mix_g, norm_ffn_g, norm_final_g, ffn_w)
    (p_l0_conv, p_l0_ssm), (p_l1_k, p_l1_v, p_l1_kidx), (p_l2_conv,), (p_l3_conv, p_l3_ssm) = new_p
    (s_l0_conv, s_l0_ssm), (s_l1_k, s_l1_v, s_l1_kidx), (s_l2_conv,), (s_l3_conv, s_l3_ssm) = new_s
    return (y_p, y_s,
            p_l0_conv, p_l0_ssm, s_l0_conv, s_l0_ssm,
            p_l1_k, p_l1_v, p_l1_kidx, s_l1_k, s_l1_v, s_l1_kidx,
            p_l2_conv, s_l2_conv,
            p_l3_conv, p_l3_ssm, s_l3_conv, s_l3_ssm)
```
